```python
import math
import jax, jax.numpy as jnp
from jax import lax
import numpy as np

D_MODEL = 1024
BATCH = 2
SEQ = 8192
DEPTH = 2
DEC_BATCH = 128
DEC_SEQ = 4
PAST_LEN = 2048
PAGE_SIZE = 128

N_EVEN = (DEPTH + 1) // 2
N_ODD = DEPTH // 2
EPS = 1e-6
NEG = -1e30

GLA_HEADS = 4
GLA_DK = 64
GLA_DV = 128
GLA_RANK = 16
GLA_TAU = 16.0
GLA_CHUNK = 16
CONV_CH = 512
CONV_WIDTH = 31
NSA_HEADS = 16
NSA_GROUPS = 4
NSA_REP = NSA_HEADS // NSA_GROUPS
NSA_DH = 64
CMP_LEN = 32
CMP_STRIDE = 16
CMP_HIDDEN = 64
SEL_BLOCK = 64
SEL_TOPN = 16
WINDOW = 512
Q_BLOCK = 128
N_MEM = 256
X_HEADS = 4
X_DH = 128
D_FF = 2816
FFN_WIDTH = 3

GLA_HK = GLA_HEADS * GLA_DK
GLA_HV = GLA_HEADS * GLA_DV
MIX_WIDTH = GLA_HV + CONV_CH
IN_A = 2 * GLA_HK + 2 * GLA_HV + GLA_RANK + 2 * CONV_CH
NSA_KVW = NSA_GROUPS * NSA_DH
IN_C = NSA_HEADS * NSA_DH + 6 * NSA_KVW + 3 * NSA_HEADS

kernel_name = 'hybrid_gla_conformer_nsa_decoder_step'


def rmsnorm(x, g):
    xf = x.astype(jnp.float32)
    y = xf * lax.rsqrt(jnp.mean(xf * xf, axis=-1, keepdims=True) + EPS) * g.astype(jnp.float32)
    return y.astype(x.dtype)


def causal_dwconv(xpad, w, b):
    ch = w.shape[1]
    y = lax.conv_general_dilated(xpad.astype(w.dtype), w[:, None, :], window_strides=(1,),
                                 padding='VALID', dimension_numbers=('NWC', 'WIO', 'NWC'),
                                 feature_group_count=ch)
    return y + b


def gla_recurrence(q, k, v, log_a, s0):
    bsz, t_len, nh, dk = q.shape
    dv = v.shape[-1]
    c = math.gcd(t_len, GLA_CHUNK)
    n = t_len // c

    def chunks(a):
        return jnp.moveaxis(a.reshape(bsz, n, c, *a.shape[2:]), 1, 0)

    tri = jnp.tril(jnp.ones((c, c), jnp.float32))

    def step(s, inp):
        qc, kc, vc, gc = inp
        b = jnp.cumsum(gc.astype(jnp.float32), axis=1)
        b_last = b[:, -1:]
        qe = qc * jnp.exp(b)
        ke = kc * jnp.exp(-b)
        kl = kc * jnp.exp(b_last - b)
        att = jnp.einsum('bihd,bjhd->bhij', qe, ke) * tri
        o = jnp.einsum('bhij,bjhv->bihv', att, vc) + jnp.einsum('bihd,bhdv->bihv', qe, s)
        s = jnp.exp(b_last[:, 0])[..., None] * s + jnp.einsum('bjhd,bjhv->bhdv', kl, vc)
        return s, o

    s, o = lax.scan(step, s0, (chunks(q), chunks(k), chunks(v), chunks(log_a)))
    o = jnp.moveaxis(o, 0, 1).reshape(bsz, t_len, nh, dv)
    return o, s


def mixer_gla_conv(xn, s0, conv_hist, w_in, w_gate, b_gate, g_out, w_dw, b_dw, g_ln, b_ln, w_out):
    bsz, t_len, _ = xn.shape
    cuts = [int(v) for v in np.cumsum([GLA_HK, GLA_HK, GLA_HV, GLA_HV, GLA_RANK])]
    q, k, v, r, a_lr, u = jnp.split(xn @ w_in, cuts, axis=-1)
    q = q.reshape(bsz, t_len, GLA_HEADS, GLA_DK) * (GLA_DK ** -0.5)
    k = k.reshape(bsz, t_len, GLA_HEADS, GLA_DK)
    v = v.reshape(bsz, t_len, GLA_HEADS, GLA_DV)
    log_a = jax.nn.log_sigmoid((a_lr @ w_gate + b_gate).astype(jnp.float32)) / GLA_TAU
    log_a = log_a.reshape(bsz, t_len, GLA_HEADS, GLA_DK)
    o, s = gla_recurrence(q, k, v, log_a, s0.astype(jnp.float32))
    o = o * lax.rsqrt(jnp.mean(o * o, axis=-1, keepdims=True) + EPS) \
        * g_out.reshape(GLA_HEADS, GLA_DV).astype(jnp.float32)
    o = (o.reshape(bsz, t_len, GLA_HV) * jax.nn.silu(r.astype(jnp.float32))).astype(xn.dtype)
    glu = u[..., :CONV_CH] * jax.nn.sigmoid(u[..., CONV_CH:])
    gpad = jnp.concatenate([conv_hist.astype(glu.dtype), glu], axis=1)
    c = causal_dwconv(gpad, w_dw, b_dw).astype(jnp.float32)
    d = c - jnp.mean(c, axis=-1, keepdims=True)
    c = d * lax.rsqrt(jnp.mean(d * d, axis=-1, keepdims=True) + EPS) * g_ln + b_ln
    c = jax.nn.silu(c).astype(xn.dtype)
    y = jnp.concatenate([o, c], axis=-1) @ w_out
    return y, s.astype(xn.dtype), gpad[:, -(CONV_WIDTH - 1):]


def masked_softmax(s, mask):
    s = jnp.where(mask, s.astype(jnp.float32), NEG)
    p = jnp.exp(s - jnp.max(s, axis=-1, keepdims=True)) * mask
    return p / jnp.maximum(jnp.sum(p, axis=-1, keepdims=True), 1e-30)


def compress(rows, pe, w1, w2):
    bsz, t_len = rows.shape[:2]
    n = t_len // CMP_STRIDE
    pieces = rows[:, :n * CMP_STRIDE].reshape(bsz, n, CMP_STRIDE, NSA_GROUPS, NSA_DH)
    blocks = jnp.concatenate([pieces[:, :-1], pieces[:, 1:]], axis=2)
    h = jax.nn.silu(jnp.einsum('bnlgd,lde->bnge', blocks + pe[:, None, :], w1))
    return jnp.einsum('bnge,ef->bngf', h, w2)


def to_sel_blocks(rows):
    bsz, t_len = rows.shape[:2]
    ns = -(-t_len // SEL_BLOCK)
    rows = jnp.pad(rows, ((0, 0), (0, ns * SEL_BLOCK - t_len), (0, 0), (0, 0)))
    return rows.reshape(bsz, ns, SEL_BLOCK, NSA_GROUPS, NSA_DH).transpose(0, 3, 1, 2, 4)


def nsa_core(q, gates, kc, vc, ks_blk, vs_blk, kw, vw, w_pos, q_pos):
    bsz, tq_len = q.shape[:2]
    nc = kc.shape[1]
    ns = ks_blk.shape[2]
    tq = q_pos[:, None]
    c_start = CMP_STRIDE * jnp.arange(nc)
    c_end = c_start + CMP_LEN - 1
    c_mask = (c_end[None, :] <= tq)[None, :, None, None, :]
    p_c = masked_softmax(jnp.einsum('btgrd,bngd->btgrn', q, kc), c_mask)
    o_c = jnp.einsum('btgrn,bngd->btgrd', p_c.astype(vc.dtype), vc)
    s_start = SEL_BLOCK * jnp.arange(ns)
    cover = ((c_start[:, None] <= s_start[None, :] + SEL_BLOCK - 1)
             & (c_end[:, None] >= s_start[None, :])).astype(jnp.float32)
    imp = jnp.einsum('btgrn,nj->btgj', p_c, cover)
    cur = q_pos // SEL_BLOCK
    j = jnp.arange(ns)[None, :]
    forced = (j == 0) | (j == cur[:, None]) | (j == cur[:, None] - 1)
    valid = s_start[None, :] <= tq
    score = jnp.where(forced[None, :, None, :], 1e6,
                      jnp.where(valid[None, :, None, :], imp, -1e6))
    _, idx = lax.top_k(score, min(SEL_TOPN, ns))
    n_sel = idx.shape[-1]
    bi = jnp.arange(bsz)[:, None, None, None]
    gi = jnp.arange(NSA_GROUPS)[None, None, :, None]
    kg = ks_blk[bi, gi, idx]
    vg = vs_blk[bi, gi, idx]
    pos = idx[..., None] * SEL_BLOCK + jnp.arange(SEL_BLOCK)
    s_mask = (pos <= q_pos[None, :, None, None, None]).reshape(bsz, tq_len, NSA_GROUPS, 1, n_sel * SEL_BLOCK)
    s_s = jnp.einsum('btgrd,btgnsd->btgrns', q, kg).reshape(bsz, tq_len, NSA_GROUPS, NSA_REP, n_sel * SEL_BLOCK)
    p_s = masked_softmax(s_s, s_mask).reshape(bsz, tq_len, NSA_GROUPS, NSA_REP, n_sel, SEL_BLOCK)
    o_s = jnp.einsum('btgrns,btgnsd->btgrd', p_s.astype(vg.dtype), vg)
    wp = w_pos[None, :]
    w_mask = ((wp <= tq) & (wp > tq - WINDOW) & (wp >= 0))[None, :, None, None, :]
    p_w = masked_softmax(jnp.einsum('btgrd,bwgd->btgrw', q, kw), w_mask)
    o_w = jnp.einsum('btgrw,bwgd->btgrd', p_w.astype(vw.dtype), vw)
    return gates[..., 0:1] * o_c + gates[..., 1:2] * o_s + gates[..., 2:3] * o_w


def nsa_project(xn, w_in):
    bsz, t_len, _ = xn.shape
    hq = NSA_HEADS * NSA_DH
    q, kv, g = jnp.split(xn @ w_in, [hq, hq + 6 * NSA_KVW], axis=-1)
    q = q.reshape(bsz, t_len, NSA_GROUPS, NSA_REP, NSA_DH) * (NSA_DH ** -0.5)
    kv = kv.reshape(bsz, t_len, 6, NSA_GROUPS, NSA_DH)
    gates = jax.nn.sigmoid(g.reshape(bsz, t_len, NSA_GROUPS, NSA_REP, 3))
    return q, kv, gates


def nsa_prompt(xn, w_in, pe, w1, w2, w_out):
    bsz, t_len, _ = xn.shape
    q, kv, gates = nsa_project(xn, w_in)
    kc = compress(kv[:, :, 0], pe[0], w1[0], w2[0])
    vc = compress(kv[:, :, 1], pe[1], w1[1], w2[1])
    ks_blk = to_sel_blocks(kv[:, :, 2])
    vs_blk = to_sel_blocks(kv[:, :, 3])
    pad = ((0, 0), (WINDOW, 0), (0, 0), (0, 0))
    kw_pad = jnp.pad(kv[:, :, 4], pad)
    vw_pad = jnp.pad(kv[:, :, 5], pad)

    def block(i):
        start = i * Q_BLOCK
        q_pos = start + jnp.arange(Q_BLOCK)
        w_pos = start - WINDOW + jnp.arange(Q_BLOCK + WINDOW)
        qb = lax.dynamic_slice_in_dim(q, start, Q_BLOCK, axis=1)
        gb = lax.dynamic_slice_in_dim(gates, start, Q_BLOCK, axis=1)
        kw = lax.dynamic_slice_in_dim(kw_pad, start, Q_BLOCK + WINDOW, axis=1)
        vw = lax.dynamic_slice_in_dim(vw_pad, start, Q_BLOCK + WINDOW, axis=1)
        return nsa_core(qb, gb, kc, vc, ks_blk, vs_blk, kw, vw, w_pos, q_pos)

    o = lax.map(block, jnp.arange(t_len // Q_BLOCK))
    o = jnp.moveaxis(o, 0, 1).reshape(bsz, t_len, NSA_HEADS * NSA_DH).astype(xn.dtype)
    win_rows = min(WINDOW, t_len)
    return o @ w_out, kv[:, :, :4], kv[:, t_len - win_rows:, 4:]


def nsa_sample(xn, pool, page_table, win_buf, w_in, pe, w1, w2, w_out):
    bsz, t_len, _ = xn.shape
    q, kv, gates = nsa_project(xn, w_in)
    past_len = page_table.shape[1] * PAGE_SIZE
    past = pool[page_table].reshape(bsz, past_len, 4, NSA_GROUPS, NSA_DH)
    full = jnp.concatenate([past.astype(kv.dtype), kv[:, :, :4]], axis=1)
    kc = compress(full[:, :, 0], pe[0], w1[0], w2[0])
    vc = compress(full[:, :, 1], pe[1], w1[1], w2[1])
    ks_blk = to_sel_blocks(full[:, :, 2])
    vs_blk = to_sel_blocks(full[:, :, 3])
    wb = win_buf.shape[1]
    wfull = jnp.concatenate([win_buf.astype(kv.dtype), kv[:, :, 4:]], axis=1)
    w_pos = past_len - wb + jnp.arange(wb + t_len)
    q_pos = past_len + jnp.arange(t_len)
    o = nsa_core(q, gates, kc, vc, ks_blk, vs_blk, wfull[:, :, 0], wfull[:, :, 1], w_pos, q_pos)
    o = o.reshape(bsz, t_len, NSA_HEADS * NSA_DH).astype(xn.dtype)
    return o @ w_out, kv[:, :, :4], wfull[:, -wb:]


def mem_kv_proj(mem, g, w_kv):
    m = rmsnorm(mem, g) @ w_kv
    return m.reshape(m.shape[0], N_MEM, 2, X_HEADS, X_DH)


def cross_attn(xn, kv, w_q, w_o):
    bsz, t_len, _ = xn.shape
    q = (xn @ w_q).reshape(bsz, t_len, X_HEADS, X_DH) * (X_DH ** -0.5)
    s = jnp.einsum('bthd,bmhd->bhtm', q, kv[:, :, 0].astype(q.dtype)).astype(jnp.float32)
    p = jax.nn.softmax(s, axis=-1).astype(xn.dtype)
    o = jnp.einsum('bhtm,bmhd->bthd', p, kv[:, :, 1].astype(xn.dtype)).reshape(bsz, t_len, X_HEADS * X_DH)
    return o @ w_o


def conv_ffn(xn, hist, w_up, w_dw, b_dw, w_down):
    u = xn @ w_up
    upad = jnp.concatenate([hist.astype(u.dtype), u], axis=1)
    c = causal_dwconv(upad, w_dw, b_dw)
    h = jax.nn.silu(c[..., :D_FF]) * c[..., D_FF:]
    return (h @ w_down).astype(xn.dtype), upad[:, -(FFN_WIDTH - 1):]


def setup_inputs(seed: int = 0) -> dict:
    key = jax.random.key(seed)
    keys = iter(jax.random.split(key, 64))
    D = D_MODEL

    def nrm(shape, scale):
        return jax.random.normal(next(keys), shape, jnp.float32) * scale

    def gain(shape):
        return 1.0 + nrm(shape, 0.02)

    n_pages = PAST_LEN // PAGE_SIZE
    n_used = DEC_BATCH * n_pages
    n_phys = n_used + n_used // 4
    page_table = jax.random.permutation(next(keys), n_phys)[:n_used].reshape(DEC_BATCH, n_pages).astype(jnp.int32)
    wb = min(WINDOW, PAST_LEN)
    return {
        'x_prompt': nrm((BATCH, SEQ, D), 1.0),
        'x_sample': nrm((DEC_BATCH, DEC_SEQ, D), 1.0),
        'cache_gla_state': nrm((N_EVEN, DEC_BATCH, GLA_HEADS, GLA_DK, GLA_DV), 1.0),
        'cache_conv': nrm((N_EVEN, DEC_BATCH, CONV_WIDTH - 1, CONV_CH), 0.5),
        'cache_nsa_kv': nrm((N_ODD, n_phys, PAGE_SIZE, 4, NSA_GROUPS, NSA_DH), 1.0),
        'cache_nsa_win': nrm((N_ODD, DEC_BATCH, wb, 2, NSA_GROUPS, NSA_DH), 1.0),
        'cache_mem_kv': nrm((DEPTH, DEC_BATCH, N_MEM, 2, X_HEADS, X_DH), 1.0),
        'cache_ffn_conv': nrm((DEPTH, DEC_BATCH, FFN_WIDTH - 1, 2 * D_FF), 0.5),
        'page_table': page_table,
        'mem_prompt': nrm((BATCH, N_MEM, D), 1.0),
        'norm_mix': gain((DEPTH, D)),
        'norm_mem': gain((DEPTH, D)),
        'norm_x': gain((DEPTH, D)),
        'norm_ffn': gain((DEPTH, D)),
        'norm_final': gain((D,)),
        'w_in_a': nrm((N_EVEN, D, IN_A), D ** -0.5),
        'w_gate_a': nrm((N_EVEN, GLA_RANK, GLA_HK), GLA_RANK ** -0.5),
        'b_gate_a': nrm((N_EVEN, GLA_HK), 0.1),
        'g_gla_out': gain((N_EVEN, GLA_HV)),
        'w_dw_b': nrm((N_EVEN, CONV_WIDTH, CONV_CH), CONV_WIDTH ** -0.5),
        'b_dw_b': nrm((N_EVEN, CONV_CH), 0.02),
        'g_ln_b': gain((N_EVEN, CONV_CH)),
        'b_ln_b': nrm((N_EVEN, CONV_CH), 0.02),
        'w_out_a': nrm((N_EVEN, MIX_WIDTH, D), MIX_WIDTH ** -0.5),
        'w_in_c': nrm((N_ODD, D, IN_C), D ** -0.5),
        'pe_cmp': nrm((N_ODD, 2, CMP_LEN, NSA_DH), 0.1),
        'w_cmp1': nrm((N_ODD, 2, CMP_LEN, NSA_DH, CMP_HIDDEN), (CMP_LEN * NSA_DH) ** -0.5),
        'w_cmp2': nrm((N_ODD, 2, CMP_HIDDEN, NSA_DH), CMP_HIDDEN ** -0.5),
        'w_out_c': nrm((N_ODD, NSA_HEADS * NSA_DH, D), (NSA_HEADS * NSA_DH) ** -0.5),
        'w_xq': nrm((DEPTH, D, X_HEADS * X_DH), D ** -0.5),
        'w_mem_kv': nrm((DEPTH, D, 2 * X_HEADS * X_DH), D ** -0.5),
        'w_xo': nrm((DEPTH, X_HEADS * X_DH, D), (X_HEADS * X_DH) ** -0.5),
        'w_up': nrm((DEPTH, D, 2 * D_FF), D ** -0.5),
        'w_ffn_dw': nrm((DEPTH, FFN_WIDTH, 2 * D_FF), FFN_WIDTH ** -0.5),
        'b_ffn_dw': nrm((DEPTH, 2 * D_FF), 0.02),
        'w_down': nrm((DEPTH, D_FF, D), D_FF ** -0.5),
    }


def reference(x_prompt, x_sample, cache_gla_state, cache_conv, cache_nsa_kv, cache_nsa_win,
              cache_mem_kv, cache_ffn_conv, page_table, mem_prompt,
              norm_mix, norm_mem, norm_x, norm_ffn, norm_final,
              w_in_a, w_gate_a, b_gate_a, g_gla_out, w_dw_b, b_dw_b, g_ln_b, b_ln_b, w_out_a,
              w_in_c, pe_cmp, w_cmp1, w_cmp2, w_out_c,
              w_xq, w_mem_kv, w_xo,
              w_up, w_ffn_dw, b_ffn_dw, w_down):
    xp, xs = x_prompt, x_sample
    bp = xp.shape[0]
    gla_p, gla_s, conv_p, conv_s = [], [], [], []
    nsa_p, nsa_s, win_p, win_s = [], [], [], []
    mem_p, ffn_p, ffn_s = [], [], []
    for l in range(DEPTH):
        i = l // 2
        hp = rmsnorm(xp, norm_mix[l])
        hs = rmsnorm(xs, norm_mix[l])
        if l % 2 == 0:
            wa = (w_in_a[i], w_gate_a[i], b_gate_a[i], g_gla_out[i], w_dw_b[i], b_dw_b[i],
                  g_ln_b[i], b_ln_b[i], w_out_a[i])
            s0 = jnp.zeros((bp, GLA_HEADS, GLA_DK, GLA_DV), jnp.float32)
            c0 = jnp.zeros((bp, CONV_WIDTH - 1, CONV_CH), xp.dtype)
            yp, sp, cp = mixer_gla_conv(hp, s0, c0, *wa)
            ys, ss, cs = mixer_gla_conv(hs, cache_gla_state[i], cache_conv[i], *wa)
            gla_p.append(sp)
            gla_s.append(ss)
            conv_p.append(cp)
            conv_s.append(cs)
        else:
            yp, kvp, wnp = nsa_prompt(hp, w_in_c[i], pe_cmp[i], w_cmp1[i], w_cmp2[i], w_out_c[i])
            ys, kvs, wns = nsa_sample(hs, cache_nsa_kv[i], page_table, cache_nsa_win[i], w_in_c[i],
                                      pe_cmp[i], w_cmp1[i], w_cmp2[i], w_out_c[i])
            nsa_p.append(kvp)
            nsa_s.append(kvs)
            win_p.append(wnp)
            win_s.append(wns)
        xp = xp + yp
        xs = xs + ys
        mkv = mem_kv_proj(mem_prompt, norm_mem[l], w_mem_kv[l])
        mem_p.append(mkv)
        xp = xp + cross_attn(rmsnorm(xp, norm_x[l]), mkv, w_xq[l], w_xo[l])
        xs = xs + cross_attn(rmsnorm(xs, norm_x[l]), cache_mem_kv[l], w_xq[l], w_xo[l])
        h0 = jnp.zeros((bp, FFN_WIDTH - 1, 2 * D_FF), xp.dtype)
        fp, hfp = conv_ffn(rmsnorm(xp, norm_ffn[l]), h0, w_up[l], w_ffn_dw[l], b_ffn_dw[l], w_down[l])
        fs, hfs = conv_ffn(rmsnorm(xs, norm_ffn[l]), cache_ffn_conv[l], w_up[l], w_ffn_dw[l], b_ffn_dw[l], w_down[l])
        xp = xp + fp
        xs = xs + fs
        ffn_p.append(hfp)
        ffn_s.append(hfs)
    y_prompt = rmsnorm(xp, norm_final)
    y_sample = rmsnorm(xs, norm_final)
    return (y_prompt, y_sample,
            jnp.stack(gla_p), jnp.stack(gla_s),
            jnp.stack(conv_p), jnp.stack(conv_s),
            jnp.stack(nsa_p), jnp.stack(nsa_s),
            jnp.stack(win_p), jnp.stack(win_s),
            jnp.stack(mem_p),
            jnp.stack(ffn_p), jnp.stack(ffn_s))
```

```python
import functools
import math

import jax
import jax.numpy as jnp
from jax import lax
from jax.experimental import pallas as pl
from jax.experimental.pallas import tpu as pltpu

EPS = 1e-6
NEG = -1e30
F32 = jnp.float32
BF = jnp.bfloat16

V7X_VMEM_BYTES = 64 * 1024 * 1024
VMEM_LIMIT = V7X_VMEM_BYTES - 8 * 1024 * 1024
LANES = 128

GLA_HEADS, GLA_DK, GLA_DV, GLA_RANK, GLA_TAU, GLA_CHUNK = 4, 64, 128, 16, 16.0, 16
CONV_CH, CONV_WIDTH = 512, 31
NSA_HEADS, NSA_GROUPS, NSA_DH = 16, 4, 64
NSA_REP = NSA_HEADS // NSA_GROUPS
CMP_LEN, CMP_STRIDE, CMP_HIDDEN = 32, 16, 64
SEL_BLOCK, SEL_TOPN, WINDOW, Q_BLOCK = 64, 16, 512, 128
X_HEADS, X_DH = 4, 128
FFN_WIDTH = 3
PAGE_SIZE = 128


def _cparams(*sem):
    return pltpu.CompilerParams(dimension_semantics=sem, vmem_limit_bytes=VMEM_LIMIT)


def _rms(x, g):
    return x * lax.rsqrt(jnp.mean(x * x, axis=-1, keepdims=True) + EPS) * g


def _sigmoid(x):
    return 1.0 / (1.0 + jnp.exp(-x))


def _silu(x):
    return x * _sigmoid(x)


def _dot(a, b):
    return jnp.dot(a, b, preferred_element_type=F32)


def _dot_nt(a, b):
    return lax.dot_general(a, b, (((1,), (1,)), ((), ())), preferred_element_type=F32)


def _dot_tn(a, b):
    return lax.dot_general(a, b, (((0,), (0,)), ((), ())), preferred_element_type=F32)


def _split3(x):
    hi = x.astype(BF)
    r1 = x - hi.astype(F32)
    mid = r1.astype(BF)
    lo = (r1 - mid.astype(F32)).astype(BF)
    return hi, mid, lo


def _norm_matmul_kernel(x_ref, g_ref, w_ref, *o_refs, splits):
    xn = _rms(x_ref[...], g_ref[...]).astype(BF)
    off = 0
    for o_ref, n in zip(o_refs, splits):
        o_ref[...] = _dot(xn, w_ref[:, off:off + n])
        off += n


def norm_matmul(x, g, w, splits, tm):
    m, d = x.shape
    n = w.shape[1]
    assert sum(splits) == n and m % tm == 0
    outs = pl.pallas_call(
        functools.partial(_norm_matmul_kernel, splits=tuple(splits)),
        grid=(m // tm,),
        in_specs=[pl.BlockSpec((tm, d), lambda i: (i, 0)),
                  pl.BlockSpec((1, d), lambda i: (0, 0)),
                  pl.BlockSpec((d, n), lambda i: (0, 0))],
        out_specs=[pl.BlockSpec((tm, s), lambda i: (i, 0)) for s in splits],
        out_shape=[jax.ShapeDtypeStruct((m, s), F32) for s in splits],
        compiler_params=_cparams("parallel"),
        name="norm_matmul",
    )(x, g.reshape(1, d), w)
    return outs


def _matmul_res_kernel(*refs, n_in):
    a_refs, w_refs = refs[:n_in], refs[n_in:2 * n_in]
    res_ref, o_ref = refs[2 * n_in], refs[2 * n_in + 1]
    acc = _dot(a_refs[0][...].astype(BF), w_refs[0][...])
    for a_ref, w_ref in zip(a_refs[1:], w_refs[1:]):
        acc = acc + _dot(a_ref[...].astype(BF), w_ref[...])
    o_ref[...] = res_ref[...] + acc


def matmul_res(a_list, w_list, res, tm):
    m, d = res.shape
    n_in = len(a_list)
    return pl.pallas_call(
        functools.partial(_matmul_res_kernel, n_in=n_in),
        grid=(m // tm,),
        in_specs=([pl.BlockSpec((tm, a.shape[1]), lambda i: (i, 0)) for a in a_list]
                  + [pl.BlockSpec(w.shape, lambda i: (0, 0)) for w in w_list]
                  + [pl.BlockSpec((tm, d), lambda i: (i, 0))]),
        out_specs=pl.BlockSpec((tm, d), lambda i: (i, 0)),
        out_shape=jax.ShapeDtypeStruct((m, d), F32),
        compiler_params=_cparams("parallel"),
        name="matmul_res",
    )(*a_list, *w_list, res)


def _gla_kernel(q_ref, k_ref, v_ref, r_ref, a_ref, wg_ref, bg_ref, go_ref, s0_ref,
                o_ref, sout_ref, s_scr, *, chunk, n_chunks):
    j = pl.program_id(1)

    @pl.when(j == 0)
    def _():
        s_scr[...] = s0_ref[...]

    ri = lax.broadcasted_iota(jnp.int32, (chunk, chunk), 0)
    ci = lax.broadcasted_iota(jnp.int32, (chunk, chunk), 1)
    tri = (ri >= ci).astype(F32)
    tri_bf = tri.astype(BF)
    eye = (lax.broadcasted_iota(jnp.int32, (GLA_DK, GLA_DK), 0)
           == lax.broadcasted_iota(jnp.int32, (GLA_DK, GLA_DK), 1)).astype(F32)

    def body(i, carry):
        sl = pl.ds(pl.multiple_of(i * chunk, chunk), chunk)
        qc = q_ref[sl, :] * (GLA_DK ** -0.5)
        kc = k_ref[sl, :]
        z = _dot(a_ref[sl, :].astype(BF), wg_ref[...]) + bg_ref[...]
        log_a = (jnp.minimum(z, 0.0) - jnp.log(1.0 + jnp.exp(-jnp.abs(z)))) / GLA_TAU
        hi, mid, lo = _split3(log_a)
        b = _dot(tri_bf, hi) + _dot(tri_bf, mid) + _dot(tri_bf, lo)
        for h in range(GLA_HEADS):
            ks = slice(h * GLA_DK, (h + 1) * GLA_DK)
            vs = slice(h * GLA_DV, (h + 1) * GLA_DV)
            bh = b[:, ks]
            b_last = bh[chunk - 1:chunk, :]
            qe = (qc[:, ks] * jnp.exp(bh)).astype(BF)
            ke = (kc[:, ks] * jnp.exp(-bh)).astype(BF)
            kl = (kc[:, ks] * jnp.exp(b_last - bh)).astype(BF)
            vh = v_ref[sl, vs].astype(BF)
            s = s_scr[h]
            att = _dot_nt(qe, ke) * tri
            o = _dot(att.astype(BF), vh) + _dot(qe, s.astype(BF))
            decay = jnp.exp(jnp.sum(eye * b_last, axis=1, keepdims=True))
            s_scr[h] = decay * s + _dot_tn(kl, vh)
            o = o * lax.rsqrt(jnp.mean(o * o, axis=-1, keepdims=True) + EPS) * go_ref[:, vs]
            o_ref[sl, vs] = o * _silu(r_ref[sl, vs])
        return carry

    lax.fori_loop(0, n_chunks, body, 0)
    sout_ref[...] = s_scr[...]


def gla(q, k, v, r, a, wg, bg, go, s0, tt):
    bsz, t_len, _ = q.shape
    chunk = math.gcd(t_len, GLA_CHUNK)
    tt = min(tt, t_len)
    hk, hv = GLA_HEADS * GLA_DK, GLA_HEADS * GLA_DV

    def tspec(w):
        return pl.BlockSpec((None, tt, w), lambda b, j: (b, j, 0))

    sspec = pl.BlockSpec((None, GLA_HEADS, GLA_DK, GLA_DV), lambda b, j: (b, 0, 0, 0))
    return pl.pallas_call(
        functools.partial(_gla_kernel, chunk=chunk, n_chunks=tt // chunk),
        grid=(bsz, t_len // tt),
        in_specs=[tspec(hk), tspec(hk), tspec(hv), tspec(hv), tspec(LANES),
                  pl.BlockSpec((LANES, hk), lambda b, j: (0, 0)),
                  pl.BlockSpec((1, hk), lambda b, j: (0, 0)),
                  pl.BlockSpec((1, hv), lambda b, j: (0, 0)),
                  sspec],
        out_specs=[tspec(hv), sspec],
        out_shape=[jax.ShapeDtypeStruct((bsz, t_len, hv), F32),
                   jax.ShapeDtypeStruct((bsz, GLA_HEADS, GLA_DK, GLA_DV), F32)],
        scratch_shapes=[pltpu.VMEM((GLA_HEADS, GLA_DK, GLA_DV), F32)],
        compiler_params=_cparams("parallel", "arbitrary"),
        name="gla",
    )(q, k, v, r, a, wg, bg, go, s0)


def _convmod_kernel(u1_ref, u2_ref, hist_ref, w_ref, b_ref, g_ref, bl_ref, c_ref, hout_ref, gpad,
                    *, tm, step, off):
    hrows = (CONV_WIDTH - 1) * step
    base = off - hrows

    @pl.when(pl.program_id(1) == 0)
    def _():
        gpad[base:off, :] = hist_ref[...]

    gpad[off:off + tm, :] = u1_ref[...] * _sigmoid(u2_ref[...])
    acc = b_ref[...] + w_ref[0:1, :] * gpad[base:base + tm, :]
    for kk in range(1, CONV_WIDTH):
        acc = acc + w_ref[kk:kk + 1, :] * gpad[base + kk * step:base + kk * step + tm, :]
    d = acc - jnp.mean(acc, axis=-1, keepdims=True)
    c = d * lax.rsqrt(jnp.mean(d * d, axis=-1, keepdims=True) + EPS) * g_ref[...] + bl_ref[...]
    c_ref[...] = _silu(c)
    new_hist = gpad[base + tm:off + tm, :]
    hout_ref[...] = new_hist
    gpad[base:off, :] = new_hist


def convmod(u, hist, w, b, g, bl, n_outer, tm, step):
    m = u.shape[0]
    rows = m // n_outer
    nt = rows // tm
    hrows = (CONV_WIDTH - 1) * step
    assert tm >= hrows or nt == 1
    off = -(-hrows // 8) * 8
    ch = CONV_CH
    vec = pl.BlockSpec((1, ch), lambda o, t: (0, 0))
    return pl.pallas_call(
        functools.partial(_convmod_kernel, tm=tm, step=step, off=off),
        grid=(n_outer, nt),
        in_specs=[pl.BlockSpec((tm, ch), lambda o, t: (o * nt + t, 0)),
                  pl.BlockSpec((tm, ch), lambda o, t: (o * nt + t, 1)),
                  pl.BlockSpec((None, hrows, ch), lambda o, t: (o, 0, 0)),
                  pl.BlockSpec((CONV_WIDTH, ch), lambda o, t: (0, 0)),
                  vec, vec, vec],
        out_specs=[pl.BlockSpec((tm, ch), lambda o, t: (o * nt + t, 0)),
                   pl.BlockSpec((None, hrows, ch), lambda o, t: (o, 0, 0))],
        out_shape=[jax.ShapeDtypeStruct((m, ch), F32),
                   jax.ShapeDtypeStruct((n_outer, hrows, ch), F32)],
        scratch_shapes=[pltpu.VMEM((off + tm, ch), F32)],
        compiler_params=_cparams("parallel", "arbitrary"),
        name="convmod",
    )(u, u, hist, w, b, g, bl)


def _xattn_kernel(q_ref, kv_ref, o_ref):
    hw = X_HEADS * X_DH
    q = q_ref[...] * (X_DH ** -0.5)
    for h in range(X_HEADS):
        sl = slice(h * X_DH, (h + 1) * X_DH)
        kh = kv_ref[:, sl].astype(BF)
        vh = kv_ref[:, hw + h * X_DH:hw + (h + 1) * X_DH].astype(BF)
        s = _dot_nt(q[:, sl].astype(BF), kh)
        p = jnp.exp(s - jnp.max(s, axis=-1, keepdims=True))
        p = p / jnp.sum(p, axis=-1, keepdims=True)
        o_ref[:, sl] = _dot(p.astype(BF), vh)


def xattn(q, kv, tq):
    bsz, t_len, hw = q.shape
    n_mem = kv.shape[1]
    return pl.pallas_call(
        _xattn_kernel,
        grid=(bsz, t_len // tq),
        in_specs=[pl.BlockSpec((None, tq, hw), lambda b, i: (b, i, 0)),
                  pl.BlockSpec((None, n_mem, 2 * hw), lambda b, i: (b, 0, 0))],
        out_specs=pl.BlockSpec((None, tq, hw), lambda b, i: (b, i, 0)),
        out_shape=jax.ShapeDtypeStruct((bsz, t_len, hw), F32),
        compiler_params=_cparams("parallel", "parallel"),
        name="xattn",
    )(q, kv)


def _ffn_kernel(*refs, tm, step, final_norm):
    (x_ref, g_ref, ha_ref, hb_ref, wua_ref, wub_ref, wda_ref, wdb_ref, ba_ref, bb_ref, wdn_ref) = refs[:11]
    rest = refs[11:]
    if final_norm:
        gf_ref, rest = rest[0], rest[1:]
    o_ref, sa_ref, sb_ref, xn_scr, acc_scr, halo_a, halo_b = rest
    t = pl.program_id(1)
    j = pl.program_id(2)
    hrows = (FFN_WIDTH - 1) * step

    @pl.when(j == 0)
    def _():
        xn_scr[...] = _rms(x_ref[...], g_ref[...]).astype(BF)

    @pl.when(t == 0)
    def _():
        halo_a[j] = ha_ref[...]
        halo_b[j] = hb_ref[...]

    def conv_half(wu_ref, wd_ref, b_ref, halo, s_ref):
        u = _dot(xn_scr[...], wu_ref[...])
        h = halo[j]
        if step % 8 == 0:
            u1 = jnp.concatenate([h[step:], u[:tm - step]], axis=0)
            u2 = jnp.concatenate([h, u[:tm - 2 * step]], axis=0)
        else:
            assert step == 1
            row = lax.broadcasted_iota(jnp.int32, u.shape, 0)
            u1 = jnp.where(row == 0, h[1:2, :], pltpu.roll(u, 1, 0))
            u2 = jnp.where(row == 0, h[0:1, :], jnp.where(row == 1, h[1:2, :], pltpu.roll(u, 2, 0)))
        c = b_ref[...] + wd_ref[0:1, :] * u2 + wd_ref[1:2, :] * u1 + wd_ref[2:3, :] * u
        new_h = u[tm - hrows:, :]
        halo[j] = new_h
        s_ref[j] = new_h
        return c

    ca = conv_half(wua_ref, wda_ref, ba_ref, halo_a, sa_ref)
    cb = conv_half(wub_ref, wdb_ref, bb_ref, halo_b, sb_ref)
    contrib = _dot((_silu(ca) * cb).astype(BF), wdn_ref[...])

    @pl.when(j == 0)
    def _():
        acc_scr[...] = x_ref[...] + contrib

    @pl.when(j > 0)
    def _():
        acc_scr[...] = acc_scr[...] + contrib

    @pl.when(j == pl.num_programs(2) - 1)
    def _():
        if final_norm:
            o_ref[...] = _rms(acc_scr[...], gf_ref[...])
        else:
            o_ref[...] = acc_scr[...]


def conv_ffn(x, g, hist, w_up, w_dw, b_dw, w_down, n_outer, tm, step, cw, final_g=None):
    m, d = x.shape
    d_ff = w_down.shape[0]
    rows = m // n_outer
    nt = rows // tm
    nj = d_ff // cw
    hrows = (FFN_WIDTH - 1) * step
    assert tm >= hrows and d_ff % cw == 0
    final_norm = final_g is not None

    in_specs = [
        pl.BlockSpec((tm, d), lambda o, t, j: (o * nt + t, 0)),
        pl.BlockSpec((1, d), lambda o, t, j: (0, 0)),
        pl.BlockSpec((None, hrows, cw), lambda o, t, j: (o, 0, j)),
        pl.BlockSpec((None, hrows, cw), lambda o, t, j: (o, 0, nj + j)),
        pl.BlockSpec((d, cw), lambda o, t, j: (0, j)),
        pl.BlockSpec((d, cw), lambda o, t, j: (0, nj + j)),
        pl.BlockSpec((FFN_WIDTH, cw), lambda o, t, j: (0, j)),
        pl.BlockSpec((FFN_WIDTH, cw), lambda o, t, j: (0, nj + j)),
        pl.BlockSpec((1, cw), lambda o, t, j: (0, j)),
        pl.BlockSpec((1, cw), lambda o, t, j: (0, nj + j)),
        pl.BlockSpec((cw, d), lambda o, t, j: (j, 0)),
    ]
    args = [x, g.reshape(1, d), hist, hist, w_up, w_up, w_dw, w_dw, b_dw.reshape(1, -1), b_dw.reshape(1, -1), w_down]
    if final_norm:
        in_specs.append(pl.BlockSpec((1, d), lambda o, t, j: (0, 0)))
        args.append(final_g.reshape(1, d))
    out, sa, sb = pl.pallas_call(
        functools.partial(_ffn_kernel, tm=tm, step=step, final_norm=final_norm),
        grid=(n_outer, nt, nj),
        in_specs=in_specs,
        out_specs=[pl.BlockSpec((tm, d), lambda o, t, j: (o * nt + t, 0)),
                   pl.BlockSpec((None, nj, hrows, cw), lambda o, t, j: (o, 0, 0, 0)),
                   pl.BlockSpec((None, nj, hrows, cw), lambda o, t, j: (o, 0, 0, 0))],
        out_shape=[jax.ShapeDtypeStruct((m, d), F32),
                   jax.ShapeDtypeStruct((n_outer, nj, hrows, cw), F32),
                   jax.ShapeDtypeStruct((n_outer, nj, hrows, cw), F32)],
        scratch_shapes=[pltpu.VMEM((tm, d), BF), pltpu.VMEM((tm, d), F32),
                        pltpu.VMEM((nj, hrows, cw), F32), pltpu.VMEM((nj, hrows, cw), F32)],
        compiler_params=_cparams("parallel", "arbitrary", "arbitrary"),
        name="conv_ffn",
    )(*args)
    def cols(z):
        return z.transpose(0, 2, 1, 3).reshape(n_outer, hrows, d_ff)

    return out, jnp.concatenate([cols(sa), cols(sb)], axis=-1)


def _compress_kernel(p_ref, pe_ref, w1_ref, w2_ref, o_ref, *, n):
    row = lax.broadcasted_iota(jnp.int32, (n, NSA_DH), 0)
    for g in range(NSA_GROUPS):
        x = p_ref[g]
        a = _dot((x + pe_ref[0:1, :]).astype(BF), w1_ref[0])
        b = _dot((x + pe_ref[1:2, :]).astype(BF), w1_ref[1])
        h = _silu(a + pltpu.roll(b, n - 1, 0))
        o = _dot(h.astype(BF), w2_ref[...])
        o_ref[g] = jnp.where(row < n - 1, o, 0.0)


def compress(pieces, pe_ab, w1_ab, w2):
    bsz, _, _, n, pw = pieces.shape
    return pl.pallas_call(
        functools.partial(_compress_kernel, n=n),
        grid=(bsz, 2),
        in_specs=[pl.BlockSpec((None, None, NSA_GROUPS, n, pw), lambda b, c: (b, c, 0, 0, 0)),
                  pl.BlockSpec((None, 2, pw), lambda b, c: (c, 0, 0)),
                  pl.BlockSpec((None, 2, pw, CMP_HIDDEN), lambda b, c: (c, 0, 0, 0)),
                  pl.BlockSpec((None, CMP_HIDDEN, NSA_DH), lambda b, c: (c, 0, 0))],
        out_specs=pl.BlockSpec((None, None, NSA_GROUPS, n, NSA_DH), lambda b, c: (b, c, 0, 0, 0)),
        out_shape=jax.ShapeDtypeStruct((bsz, 2, NSA_GROUPS, n, NSA_DH), F32),
        compiler_params=_cparams("parallel", "parallel"),
        name="nsa_compress",
    )(pieces, pe_ab, w1_ab, w2)


def _gather_kernel(pt_ref, page_ref, o_ref):
    o_ref[...] = page_ref[...]


def gather_pages(pool, page_table):
    bsz, n_pages = page_table.shape
    _, ps, w = pool.shape
    return pl.pallas_call(
        _gather_kernel,
        grid_spec=pltpu.PrefetchScalarGridSpec(
            num_scalar_prefetch=1,
            grid=(bsz, n_pages),
            in_specs=[pl.BlockSpec((None, ps, w), lambda b, p, pt: (pt[b, p], 0, 0))],
            out_specs=pl.BlockSpec((None, ps, w), lambda b, p, pt: (b, p, 0)),
        ),
        out_shape=jax.ShapeDtypeStruct((bsz, n_pages * ps, w), pool.dtype),
        compiler_params=_cparams("parallel", "arbitrary"),
        name="gather_pages",
    )(page_table, pool)


def _masked_softmax(s, mask):
    s = jnp.where(mask, s, NEG)
    p = jnp.exp(s - jnp.max(s, axis=-1, keepdims=True)) * mask.astype(F32)
    return p / jnp.maximum(jnp.sum(p, axis=-1, keepdims=True), 1e-30)


def _nsa_attn_kernel(q_ref, gt_ref, kc_ref, vc_ref, ks_ref, vs_ref, kw_ref, vw_ref, o_ref,
                     *, tq, ncp, ns, ck, nw, prompt, q_base, w_base):
    rep, dh = NSA_REP, NSA_DH
    rows = rep * tq
    if prompt:
        qb = pl.program_id(2) * tq
        n_chunks = (qb + tq - 1) // ck + 1
        ws = pl.multiple_of(jnp.maximum(qb - WINDOW, 0), tq)
        wb = ws
    else:
        qb, n_chunks, ws, wb = q_base, 1, 0, w_base

    q = q_ref[...] * (dh ** -0.5)
    q4 = jnp.concatenate([q[:, r * dh:(r + 1) * dh] for r in range(rep)], axis=0).astype(BF)

    def tpos(shape):
        return (lax.broadcasted_iota(jnp.int32, shape, 0) & (tq - 1)) + qb

    def stack(x):
        return jnp.concatenate([x] * rep, axis=0)

    s = _dot_nt(q4, kc_ref[...].astype(BF))
    n_idx = lax.broadcasted_iota(jnp.int32, (rows, ncp), 1)
    c_mask = (CMP_STRIDE * n_idx + CMP_LEN - 1) <= tpos((rows, ncp))
    p_c = _masked_softmax(s, c_mask)
    o_c = _dot(p_c.astype(BF), vc_ref[...].astype(BF))

    psum = p_c[0:tq]
    for r in range(1, rep):
        psum = psum + p_c[r * tq:(r + 1) * tq]
    cn = lax.broadcasted_iota(jnp.int32, (ncp, LANES), 0) * CMP_STRIDE
    cj = lax.broadcasted_iota(jnp.int32, (ncp, LANES), 1) * SEL_BLOCK
    cover = jnp.where((cn <= cj + SEL_BLOCK - 1) & (cn + CMP_LEN - 1 >= cj), 1.0, 0.0).astype(BF)
    hi, mid, lo = _split3(psum)
    imp = _dot(hi, cover) + _dot(mid, cover) + _dot(lo, cover)

    jj = lax.broadcasted_iota(jnp.int32, (tq, LANES), 1)
    tp = tpos((tq, LANES))
    cur = tp // SEL_BLOCK
    forced = (jj == 0) | (jj == cur) | (jj == cur - 1)
    valid = jj * SEL_BLOCK <= tp
    score = jnp.where(forced, 1e6, jnp.where(valid, imp, -1e6))
    score = jnp.where(jj < ns, score, -3e6)
    sel = jnp.zeros((tq, LANES), F32)
    for _ in range(min(SEL_TOPN, ns)):
        mx = jnp.max(score, axis=-1, keepdims=True)
        first = jnp.min(jnp.where(score == mx, jj, LANES), axis=-1, keepdims=True)
        pick = jj == first
        sel = jnp.where(pick, 1.0, sel)
        score = jnp.where(pick, -3e38, score)
    sel_bf = sel.astype(BF)

    def sel_chunk(c, carry):
        m, l, acc = carry
        k0 = pl.multiple_of(c * ck, ck)
        kb = ks_ref[pl.ds(k0, ck), :]
        vb = vs_ref[pl.ds(k0, ck), :]
        ej = lax.broadcasted_iota(jnp.int32, (LANES, ck), 0)
        ek = lax.broadcasted_iota(jnp.int32, (LANES, ck), 1) + k0
        expand = jnp.where(ek // SEL_BLOCK == ej, 1.0, 0.0).astype(BF)
        mk = _dot(sel_bf, expand)
        kpos = lax.broadcasted_iota(jnp.int32, (tq, ck), 1) + k0
        mk = jnp.where(kpos <= tpos((tq, ck)), mk, 0.0)
        mk4 = stack(mk)
        sc = jnp.where(mk4 > 0.0, _dot_nt(q4, kb), NEG)
        m_new = jnp.maximum(m, jnp.max(sc, axis=-1, keepdims=True))
        alpha = jnp.exp(m - m_new)
        p = jnp.exp(sc - m_new) * mk4
        l = alpha * l + jnp.sum(p, axis=-1, keepdims=True)
        acc = alpha * acc + _dot(p.astype(BF), vb)
        return m_new, l, acc

    init = (jnp.full((rows, 1), NEG, F32), jnp.zeros((rows, 1), F32), jnp.zeros((rows, dh), F32))
    _, l_s, acc_s = lax.fori_loop(0, n_chunks, sel_chunk, init)
    o_s = acc_s / jnp.maximum(l_s, 1e-30)

    kwb = kw_ref[pl.ds(ws, nw), :]
    vwb = vw_ref[pl.ds(ws, nw), :]
    wp = lax.broadcasted_iota(jnp.int32, (rows, nw), 1) + wb
    tw = tpos((rows, nw))
    w_mask = (wp <= tw) & (wp > tw - WINDOW) & (wp >= 0)
    p_w = _masked_softmax(_dot_nt(q4, kwb), w_mask)
    o_w = _dot(p_w.astype(BF), vwb)

    gt = _sigmoid(gt_ref[...])

    def gate(br):
        return jnp.concatenate([gt[:, r * 3 + br:r * 3 + br + 1] for r in range(rep)], axis=0)

    o4 = gate(0) * o_c + gate(1) * o_s + gate(2) * o_w
    o_ref[...] = jnp.concatenate([o4[r * tq:(r + 1) * tq] for r in range(rep)], axis=1)


def nsa_attn(q, gates, kcvc, ks, vs, kw, vw, *, tq, ns, ck, nw, prompt, q_base=0, w_base=0):
    bsz, t_len, _ = q.shape
    ncp = kcvc.shape[3]
    tk, tw = ks.shape[2], kw.shape[2]
    gw = NSA_REP * NSA_DH

    def kv_spec(n):
        return pl.BlockSpec((None, None, n, NSA_DH), lambda b, g, i: (b, g, 0, 0))

    return pl.pallas_call(
        functools.partial(_nsa_attn_kernel, tq=tq, ncp=ncp, ns=ns, ck=ck, nw=nw, prompt=prompt,
                          q_base=q_base, w_base=w_base),
        grid=(bsz, NSA_GROUPS, t_len // tq),
        in_specs=[pl.BlockSpec((None, tq, gw), lambda b, g, i: (b, i, g)),
                  pl.BlockSpec((None, None, tq, NSA_REP * 3), lambda b, g, i: (b, g, i, 0)),
                  pl.BlockSpec((None, None, None, ncp, NSA_DH), lambda b, g, i: (b, 0, g, 0, 0)),
                  pl.BlockSpec((None, None, None, ncp, NSA_DH), lambda b, g, i: (b, 1, g, 0, 0)),
                  kv_spec(tk), kv_spec(tk), kv_spec(tw), kv_spec(tw)],
        out_specs=pl.BlockSpec((None, tq, gw), lambda b, g, i: (b, i, g)),
        out_shape=jax.ShapeDtypeStruct((bsz, t_len, NSA_GROUPS * gw), F32),
        compiler_params=_cparams("parallel", "parallel", "arbitrary"),
        name="nsa_attn",
    )(q, gates, kcvc, kcvc, ks, vs, kw, vw)


def _pad_cols(w, n):
    return jnp.pad(w, ((0, 0), (0, n - w.shape[1])))


def _group_major(x, bsz, t_len):
    return x.reshape(bsz, t_len, NSA_GROUPS, NSA_DH).transpose(0, 2, 1, 3)


def _pad_rows(x, n):
    return jnp.pad(x, ((0, 0), (0, 0), (0, n - x.shape[2]), (0, 0)))


def _to_time_major(x, bsz, t_len):
    return x.reshape(bsz, t_len, -1).transpose(1, 0, 2).reshape(t_len * bsz, -1)


def _to_batch_major(x, bsz, t_len):
    return x.reshape(t_len, bsz, -1).transpose(1, 0, 2).reshape(bsz * t_len, -1)


def kernel(x_prompt, x_sample, cache_gla_state, cache_conv, cache_nsa_kv, cache_nsa_win, cache_mem_kv,
           cache_ffn_conv, page_table, mem_prompt, norm_mix, norm_mem, norm_x, norm_ffn, norm_final,
           w_in_a, w_gate_a, b_gate_a, g_gla_out, w_dw_b, b_dw_b, g_ln_b, b_ln_b, w_out_a,
           w_in_c, pe_cmp, w_cmp1, w_cmp2, w_out_c, w_xq, w_mem_kv, w_xo, w_up, w_ffn_dw, b_ffn_dw, w_down):
    bp, tp, d = x_prompt.shape
    bs, ts, _ = x_sample.shape
    depth = norm_mix.shape[0]
    n_mem = mem_prompt.shape[1]
    d_ff = w_down.shape[1]
    hk, hv = GLA_HEADS * GLA_DK, GLA_HEADS * GLA_DV
    kvw = NSA_GROUPS * NSA_DH
    hq = NSA_HEADS * NSA_DH
    past_len = page_table.shape[1] * PAGE_SIZE

    xp = x_prompt.reshape(bp * tp, d)
    xs = x_sample.reshape(bs * ts, d)
    tm_p = 512
    tm_s = bs * ts

    outs = {k: [] for k in ("gla_p", "gla_s", "conv_p", "conv_s", "nsa_p", "nsa_s", "win_p", "win_s",
                            "mem_p", "ffn_p", "ffn_s")}
    for l in range(depth):
        i = l // 2
        if l % 2 == 0:
            wi = w_in_a[i]
            c0, c1, c2, c3, c4 = hk, 2 * hk, 2 * hk + hv, 2 * hk + 2 * hv, 2 * hk + 2 * hv + GLA_RANK
            w_a = jnp.concatenate([wi[:, c4:], wi[:, c1:c2], wi[:, c2:c3], wi[:, :c0], wi[:, c0:c1],
                                   _pad_cols(wi[:, c3:c4], LANES)], axis=1).astype(BF)
            splits = (2 * CONV_CH, hv, hv, hk, hk, LANES)
            wg = jnp.pad(w_gate_a[i], ((0, LANES - GLA_RANK), (0, 0))).astype(BF)
            bg = b_gate_a[i].reshape(1, hk)
            go = g_gla_out[i].reshape(1, hv)
            w_out = w_out_a[i].astype(BF)
            vecs = [v.reshape(1, CONV_CH) for v in (b_dw_b[i], g_ln_b[i], b_ln_b[i])]

            def mixer(x, bsz, t_len, tm, s0, hist_tm, n_outer, ctm, step):
                u, v, r, q, k, a = norm_matmul(x, norm_mix[l], w_a, splits, tm)
                sh = lambda z: z.reshape(bsz, t_len, -1)
                og, s_new = gla(sh(q), sh(k), sh(v), sh(r), sh(a), wg, bg, go, s0, 1024)
                if step > 1:
                    u = _to_time_major(u, bsz, t_len)
                c, hist_new = convmod(u, hist_tm, w_dw_b[i], *vecs, n_outer, ctm, step)
                if step > 1:
                    c = _to_batch_major(c, bsz, t_len)
                y = matmul_res([og.reshape(bsz * t_len, hv), c], [w_out[:hv], w_out[hv:]], x, tm)
                return y, s_new, hist_new

            xp, sp, cp = mixer(xp, bp, tp, tm_p, jnp.zeros((bp, GLA_HEADS, GLA_DK, GLA_DV), F32),
                               jnp.zeros((bp, CONV_WIDTH - 1, CONV_CH), F32), bp, 512, 1)
            hist_s = cache_conv[i].transpose(1, 0, 2).reshape(1, (CONV_WIDTH - 1) * bs, CONV_CH)
            xs, ss, cs = mixer(xs, bs, ts, tm_s, cache_gla_state[i], hist_s, 1, bs * ts, bs)
            cs = cs.reshape(CONV_WIDTH - 1, bs, CONV_CH).transpose(1, 0, 2)
            outs["gla_p"].append(sp)
            outs["gla_s"].append(ss)
            outs["conv_p"].append(cp)
            outs["conv_s"].append(cs)
        else:
            n_in = hq + 6 * kvw + 3 * NSA_HEADS
            n_pad = -(-n_in // LANES) * LANES
            w_c = _pad_cols(w_in_c[i], n_pad).astype(BF)
            splits = (hq, 4 * kvw, 2 * kvw, n_pad - hq - 6 * kvw)
            w_out = w_out_c[i].astype(BF)
            pw = CMP_STRIDE * NSA_DH
            pe_ab = pe_cmp[i].reshape(2, 2, pw)
            w1_ab = w_cmp1[i].reshape(2, 2, pw, CMP_HIDDEN).astype(BF)
            w2 = w_cmp2[i].astype(BF)

            def gates_gm(gl, bsz, t_len):
                return gl[:, :3 * NSA_HEADS].reshape(bsz, t_len, NSA_GROUPS, NSA_REP * 3).transpose(0, 2, 1, 3)

            def pieces_of(rows, bsz, n):
                r6 = rows[:, :n * CMP_STRIDE].reshape(bsz, n, CMP_STRIDE, 2, NSA_GROUPS, NSA_DH)
                return r6.transpose(0, 3, 4, 1, 2, 5).reshape(bsz, 2, NSA_GROUPS, n, pw)

            q, kv4, win, gl = norm_matmul(xp, norm_mix[l], w_c, splits, tm_p)
            kv4b = kv4.reshape(bp, tp, 4 * kvw)
            kcvc = compress(pieces_of(kv4b[:, :, :2 * kvw], bp, tp // CMP_STRIDE), pe_ab, w1_ab, w2)
            ks = _group_major(kv4[:, 2 * kvw:3 * kvw], bp, tp).astype(BF)
            vs = _group_major(kv4[:, 3 * kvw:], bp, tp).astype(BF)
            kw = _group_major(win[:, :kvw], bp, tp).astype(BF)
            vw = _group_major(win[:, kvw:], bp, tp).astype(BF)
            o = nsa_attn(q.reshape(bp, tp, hq), gates_gm(gl, bp, tp), kcvc, ks, vs, kw, vw,
                         tq=Q_BLOCK, ns=-(-tp // SEL_BLOCK), ck=512, nw=WINDOW + Q_BLOCK, prompt=True)
            xp = matmul_res([o.reshape(bp * tp, hq)], [w_out], xp, tm_p)
            win_rows = min(WINDOW, tp)
            outs["nsa_p"].append(kv4.reshape(bp, tp, 4, NSA_GROUPS, NSA_DH))
            outs["win_p"].append(win.reshape(bp, tp, 2, NSA_GROUPS, NSA_DH)[:, tp - win_rows:])

            q, kv4, win, gl = norm_matmul(xs, norm_mix[l], w_c, splits, tm_s)
            pool = cache_nsa_kv[i].reshape(cache_nsa_kv.shape[1], PAGE_SIZE, 4 * kvw)
            past = gather_pages(pool, page_table)
            kv4b = kv4.reshape(bs, ts, 4 * kvw)
            full = jnp.concatenate([past, kv4b], axis=1)
            t_full = past_len + ts
            n_pieces = t_full // CMP_STRIDE
            kcvc = compress(pieces_of(full[:, :, :2 * kvw], bs, n_pieces), pe_ab, w1_ab, w2)
            ns = -(-t_full // SEL_BLOCK)
            tk_pad = ns * SEL_BLOCK
            ks = _pad_rows(_group_major(full[:, :, 2 * kvw:3 * kvw], bs, t_full), tk_pad).astype(BF)
            vs = _pad_rows(_group_major(full[:, :, 3 * kvw:], bs, t_full), tk_pad).astype(BF)
            wb = cache_nsa_win.shape[2]
            wfull = jnp.concatenate([cache_nsa_win[i].reshape(bs, wb, 2 * kvw), win.reshape(bs, ts, 2 * kvw)], axis=1)
            tw_pad = -(-(wb + ts) // 16) * 16
            kw = _pad_rows(_group_major(wfull[:, :, :kvw], bs, wb + ts), tw_pad).astype(BF)
            vw = _pad_rows(_group_major(wfull[:, :, kvw:], bs, wb + ts), tw_pad).astype(BF)
            o = nsa_attn(q.reshape(bs, ts, hq), gates_gm(gl, bs, ts), kcvc, ks, vs, kw, vw,
                         tq=ts, ns=ns, ck=tk_pad, nw=tw_pad, prompt=False, q_base=past_len,
                         w_base=past_len - wb)
            xs = matmul_res([o.reshape(bs * ts, hq)], [w_out], xs, tm_s)
            outs["nsa_s"].append(kv4.reshape(bs, ts, 4, NSA_GROUPS, NSA_DH))
            outs["win_s"].append(wfull[:, ts:].reshape(bs, wb, 2, NSA_GROUPS, NSA_DH))

        xhw = X_HEADS * X_DH
        (mkv,) = norm_matmul(mem_prompt.reshape(bp * n_mem, d), norm_mem[l], w_mem_kv[l].astype(BF),
                             (2 * xhw,), bp * n_mem)
        outs["mem_p"].append(mkv.reshape(bp, n_mem, 2, X_HEADS, X_DH))
        w_q = w_xq[l].astype(BF)
        w_o = w_xo[l].astype(BF)
        (qx,) = norm_matmul(xp, norm_x[l], w_q, (xhw,), tm_p)
        ox = xattn(qx.reshape(bp, tp, xhw), mkv.reshape(bp, n_mem, 2 * xhw), 512)
        xp = matmul_res([ox.reshape(bp * tp, xhw)], [w_o], xp, tm_p)
        (qx,) = norm_matmul(xs, norm_x[l], w_q, (xhw,), tm_s)
        ox = xattn(qx.reshape(bs, ts, xhw), cache_mem_kv[l].reshape(bs, n_mem, 2 * xhw), ts)
        xs = matmul_res([ox.reshape(bs * ts, xhw)], [w_o], xs, tm_s)

        wu = w_up[l].astype(BF)
        wd = w_down[l].astype(BF)
        fg = norm_final if l == depth - 1 else None
        xp, hfp = conv_ffn(xp, norm_ffn[l], jnp.zeros((bp, FFN_WIDTH - 1, 2 * d_ff), F32), wu, w_ffn_dw[l],
                           b_ffn_dw[l], wd, bp, 512, 1, 256, fg)
        hist_s = cache_ffn_conv[l].transpose(1, 0, 2).reshape(1, (FFN_WIDTH - 1) * bs, 2 * d_ff)
        xs_tm, hfs = conv_ffn(_to_time_major(xs, bs, ts), norm_ffn[l], hist_s, wu, w_ffn_dw[l], b_ffn_dw[l], wd,
                              1, bs * ts, bs, 256, fg)
        xs = _to_batch_major(xs_tm, bs, ts)
        outs["ffn_p"].append(hfp)
        outs["ffn_s"].append(hfs.reshape(FFN_WIDTH - 1, bs, 2 * d_ff).transpose(1, 0, 2))

    st = jnp.stack
    return (xp.reshape(bp, tp, d), xs.reshape(bs, ts, d),
            st(outs["gla_p"]), st(outs["gla_s"]), st(outs["conv_p"]), st(outs["conv_s"]),
            st(outs["nsa_p"]), st(outs["nsa_s"]), st(outs["win_p"]), st(outs["win_s"]),
            st(outs["mem_p"]), st(outs["ffn_p"]), st(outs["ffn_s"]))
```

```python
import functools
import math

import jax
import jax.numpy as jnp
from jax import lax
from jax.experimental import pallas as pl
from jax.experimental.pallas import tpu as pltpu

EPS = 1e-6
NEG = -1e30
F32 = jnp.float32
BF = jnp.bfloat16

V7X_VMEM_BYTES = 64 * 1024 * 1024
VMEM_LIMIT = V7X_VMEM_BYTES - 8 * 1024 * 1024
LANES = 128

GLA_HEADS, GLA_DK, GLA_DV, GLA_RANK, GLA_TAU, GLA_CHUNK = 4, 64, 128, 16, 16.0, 16
CONV_CH, CONV_WIDTH = 512, 31
NSA_HEADS, NSA_GROUPS, NSA_DH = 16, 4, 64
NSA_REP = NSA_HEADS // NSA_GROUPS
CMP_LEN, CMP_STRIDE, CMP_HIDDEN = 32, 16, 64
SEL_BLOCK, SEL_TOPN, WINDOW, Q_BLOCK = 64, 16, 512, 128
X_HEADS, X_DH = 4, 128
FFN_WIDTH = 3
PAGE_SIZE = 128


def _cparams(*sem):
    return pltpu.CompilerParams(dimension_semantics=sem, vmem_limit_bytes=VMEM_LIMIT)


def _rms(x, g):
    return x * lax.rsqrt(jnp.mean(x * x, axis=-1, keepdims=True) + EPS) * g


def _sigmoid(x):
    return 1.0 / (1.0 + jnp.exp(-x))


def _silu(x):
    return x * _sigmoid(x)


def _dot(a, b):
    return jnp.dot(a, b, preferred_element_type=F32)


def _dot_nt(a, b):
    return lax.dot_general(a, b, (((1,), (1,)), ((), ())), preferred_element_type=F32)


def _dot_tn(a, b):
    return lax.dot_general(a, b, (((0,), (0,)), ((), ())), preferred_element_type=F32)


def _split3(x):
    hi = x.astype(BF)
    r1 = x - hi.astype(F32)
    mid = r1.astype(BF)
    lo = (r1 - mid.astype(F32)).astype(BF)
    return hi, mid, lo


def _norm_matmul_kernel(x_ref, g_ref, w_ref, *o_refs, splits):
    xn = _rms(x_ref[...], g_ref[...]).astype(BF)
    off = 0
    for o_ref, n in zip(o_refs, splits):
        o_ref[...] = _dot(xn, w_ref[:, off:off + n])
        off += n


def norm_matmul(x, g, w, splits, tm):
    m, d = x.shape
    n = w.shape[1]
    assert sum(splits) == n and m % tm == 0
    outs = pl.pallas_call(
        functools.partial(_norm_matmul_kernel, splits=tuple(splits)),
        grid=(m // tm,),
        in_specs=[pl.BlockSpec((tm, d), lambda i: (i, 0)),
                  pl.BlockSpec((1, d), lambda i: (0, 0)),
                  pl.BlockSpec((d, n), lambda i: (0, 0))],
        out_specs=[pl.BlockSpec((tm, s), lambda i: (i, 0)) for s in splits],
        out_shape=[jax.ShapeDtypeStruct((m, s), F32) for s in splits],
        compiler_params=_cparams("parallel"),
        name="norm_matmul",
    )(x, g.reshape(1, d), w)
    return outs


def _matmul_res_kernel(*refs, n_in):
    a_refs, w_refs = refs[:n_in], refs[n_in:2 * n_in]
    res_ref, o_ref = refs[2 * n_in], refs[2 * n_in + 1]
    acc = _dot(a_refs[0][...].astype(BF), w_refs[0][...])
    for a_ref, w_ref in zip(a_refs[1:], w_refs[1:]):
        acc = acc + _dot(a_ref[...].astype(BF), w_ref[...])
    o_ref[...] = res_ref[...] + acc


def matmul_res(a_list, w_list, res, tm):
    m, d = res.shape
    n_in = len(a_list)
    return pl.pallas_call(
        functools.partial(_matmul_res_kernel, n_in=n_in),
        grid=(m // tm,),
        in_specs=([pl.BlockSpec((tm, a.shape[1]), lambda i: (i, 0)) for a in a_list]
                  + [pl.BlockSpec(w.shape, lambda i: (0, 0)) for w in w_list]
                  + [pl.BlockSpec((tm, d), lambda i: (i, 0))]),
        out_specs=pl.BlockSpec((tm, d), lambda i: (i, 0)),
        out_shape=jax.ShapeDtypeStruct((m, d), F32),
        compiler_params=_cparams("parallel"),
        name="matmul_res",
    )(*a_list, *w_list, res)


def _gla_kernel(q_ref, k_ref, v_ref, r_ref, a_ref, wg_ref, bg_ref, go_ref, s0_ref,
                o_ref, sout_ref, s_scr, *, chunk, n_chunks):
    j = pl.program_id(1)

    @pl.when(j == 0)
    def _():
        s_scr[...] = s0_ref[...]

    ri = lax.broadcasted_iota(jnp.int32, (chunk, chunk), 0)
    ci = lax.broadcasted_iota(jnp.int32, (chunk, chunk), 1)
    tri = (ri >= ci).astype(F32)
    tri_bf = tri.astype(BF)
    eye = (lax.broadcasted_iota(jnp.int32, (GLA_DK, GLA_DK), 0)
           == lax.broadcasted_iota(jnp.int32, (GLA_DK, GLA_DK), 1)).astype(F32)

    def body(i, carry):
        sl = pl.ds(pl.multiple_of(i * chunk, chunk), chunk)
        qc = q_ref[sl, :] * (GLA_DK ** -0.5)
        kc = k_ref[sl, :]
        z = _dot(a_ref[sl, :].astype(BF), wg_ref[...]) + bg_ref[...]
        log_a = (jnp.minimum(z, 0.0) - jnp.log(1.0 + jnp.exp(-jnp.abs(z)))) / GLA_TAU
        hi, mid, lo = _split3(log_a)
        b = _dot(tri_bf, hi) + _dot(tri_bf, mid) + _dot(tri_bf, lo)
        for h in range(GLA_HEADS):
            ks = slice(h * GLA_DK, (h + 1) * GLA_DK)
            vs = slice(h * GLA_DV, (h + 1) * GLA_DV)
            bh = b[:, ks]
            b_last = bh[chunk - 1:chunk, :]
            qe = (qc[:, ks] * jnp.exp(bh)).astype(BF)
            ke = (kc[:, ks] * jnp.exp(-bh)).astype(BF)
            kl = (kc[:, ks] * jnp.exp(b_last - bh)).astype(BF)
            vh = v_ref[sl, vs].astype(BF)
            s = s_scr[h]
            att = _dot_nt(qe, ke) * tri
            o = _dot(att.astype(BF), vh) + _dot(qe, s.astype(BF))
            decay = jnp.exp(jnp.sum(eye * b_last, axis=1, keepdims=True))
            s_scr[h] = decay * s + _dot_tn(kl, vh)
            o = o * lax.rsqrt(jnp.mean(o * o, axis=-1, keepdims=True) + EPS) * go_ref[:, vs]
            o_ref[sl, vs] = o * _silu(r_ref[sl, vs])
        return carry

    lax.fori_loop(0, n_chunks, body, 0)
    sout_ref[...] = s_scr[...]


def gla(q, k, v, r, a, wg, bg, go, s0, tt):
    bsz, t_len, _ = q.shape
    chunk = math.gcd(t_len, GLA_CHUNK)
    tt = min(tt, t_len)
    hk, hv = GLA_HEADS * GLA_DK, GLA_HEADS * GLA_DV

    def tspec(w):
        return pl.BlockSpec((None, tt, w), lambda b, j: (b, j, 0))

    sspec = pl.BlockSpec((None, GLA_HEADS, GLA_DK, GLA_DV), lambda b, j: (b, 0, 0, 0))
    return pl.pallas_call(
        functools.partial(_gla_kernel, chunk=chunk, n_chunks=tt // chunk),
        grid=(bsz, t_len // tt),
        in_specs=[tspec(hk), tspec(hk), tspec(hv), tspec(hv), tspec(LANES),
                  pl.BlockSpec((LANES, hk), lambda b, j: (0, 0)),
                  pl.BlockSpec((1, hk), lambda b, j: (0, 0)),
                  pl.BlockSpec((1, hv), lambda b, j: (0, 0)),
                  sspec],
        out_specs=[tspec(hv), sspec],
        out_shape=[jax.ShapeDtypeStruct((bsz, t_len, hv), F32),
                   jax.ShapeDtypeStruct((bsz, GLA_HEADS, GLA_DK, GLA_DV), F32)],
        scratch_shapes=[pltpu.VMEM((GLA_HEADS, GLA_DK, GLA_DV), F32)],
        compiler_params=_cparams("parallel", "arbitrary"),
        name="gla",
    )(q, k, v, r, a, wg, bg, go, s0)


def _convmod_kernel(u1_ref, u2_ref, hist_ref, w_ref, b_ref, g_ref, bl_ref, c_ref, hout_ref, gpad,
                    *, tm, step, off):
    hrows = (CONV_WIDTH - 1) * step
    base = off - hrows

    @pl.when(pl.program_id(1) == 0)
    def _():
        gpad[base:off, :] = hist_ref[...]

    gpad[off:off + tm, :] = u1_ref[...] * _sigmoid(u2_ref[...])
    acc = b_ref[...] + w_ref[0:1, :] * gpad[base:base + tm, :]
    for kk in range(1, CONV_WIDTH):
        acc = acc + w_ref[kk:kk + 1, :] * gpad[base + kk * step:base + kk * step + tm, :]
    d = acc - jnp.mean(acc, axis=-1, keepdims=True)
    c = d * lax.rsqrt(jnp.mean(d * d, axis=-1, keepdims=True) + EPS) * g_ref[...] + bl_ref[...]
    c_ref[...] = _silu(c)
    new_hist = gpad[base + tm:off + tm, :]
    hout_ref[...] = new_hist
    gpad[base:off, :] = new_hist


def convmod(u, hist, w, b, g, bl, n_outer, tm, step):
    m = u.shape[0]
    rows = m // n_outer
    nt = rows // tm
    hrows = (CONV_WIDTH - 1) * step
    assert tm >= hrows or nt == 1
    off = -(-hrows // 8) * 8
    ch = CONV_CH
    vec = pl.BlockSpec((1, ch), lambda o, t: (0, 0))
    return pl.pallas_call(
        functools.partial(_convmod_kernel, tm=tm, step=step, off=off),
        grid=(n_outer, nt),
        in_specs=[pl.BlockSpec((tm, ch), lambda o, t: (o * nt + t, 0)),
                  pl.BlockSpec((tm, ch), lambda o, t: (o * nt + t, 1)),
                  pl.BlockSpec((None, hrows, ch), lambda o, t: (o, 0, 0)),
                  pl.BlockSpec((CONV_WIDTH, ch), lambda o, t: (0, 0)),
                  vec, vec, vec],
        out_specs=[pl.BlockSpec((tm, ch), lambda o, t: (o * nt + t, 0)),
                   pl.BlockSpec((None, hrows, ch), lambda o, t: (o, 0, 0))],
        out_shape=[jax.ShapeDtypeStruct((m, ch), F32),
                   jax.ShapeDtypeStruct((n_outer, hrows, ch), F32)],
        scratch_shapes=[pltpu.VMEM((off + tm, ch), F32)],
        compiler_params=_cparams("parallel", "arbitrary"),
        name="convmod",
    )(u, u, hist, w, b, g, bl)


def _xattn_kernel(q_ref, kv_ref, o_ref):
    hw = X_HEADS * X_DH
    q = q_ref[...] * (X_DH ** -0.5)
    for h in range(X_HEADS):
        sl = slice(h * X_DH, (h + 1) * X_DH)
        kh = kv_ref[:, sl].astype(BF)
        vh = kv_ref[:, hw + h * X_DH:hw + (h + 1) * X_DH].astype(BF)
        s = _dot_nt(q[:, sl].astype(BF), kh)
        p = jnp.exp(s - jnp.max(s, axis=-1, keepdims=True))
        p = p / jnp.sum(p, axis=-1, keepdims=True)
        o_ref[:, sl] = _dot(p.astype(BF), vh)


def xattn(q, kv, tq):
    bsz, t_len, hw = q.shape
    n_mem = kv.shape[1]
    return pl.pallas_call(
        _xattn_kernel,
        grid=(bsz, t_len // tq),
        in_specs=[pl.BlockSpec((None, tq, hw), lambda b, i: (b, i, 0)),
                  pl.BlockSpec((None, n_mem, 2 * hw), lambda b, i: (b, 0, 0))],
        out_specs=pl.BlockSpec((None, tq, hw), lambda b, i: (b, i, 0)),
        out_shape=jax.ShapeDtypeStruct((bsz, t_len, hw), F32),
        compiler_params=_cparams("parallel", "parallel"),
        name="xattn",
    )(q, kv)


def _ffn_kernel(*refs, tm, step, final_norm):
    (x_ref, g_ref, ha_ref, hb_ref, wua_ref, wub_ref, wda_ref, wdb_ref, ba_ref, bb_ref, wdn_ref) = refs[:11]
    rest = refs[11:]
    if final_norm:
        gf_ref, rest = rest[0], rest[1:]
    o_ref, sa_ref, sb_ref, xn_scr, acc_scr, halo_a, halo_b = rest
    t = pl.program_id(1)
    j = pl.program_id(2)
    hrows = (FFN_WIDTH - 1) * step

    @pl.when(j == 0)
    def _():
        xn_scr[...] = _rms(x_ref[...], g_ref[...]).astype(BF)

    @pl.when(t == 0)
    def _():
        halo_a[j] = ha_ref[...]
        halo_b[j] = hb_ref[...]

    def conv_half(wu_ref, wd_ref, b_ref, halo, s_ref):
        u = _dot(xn_scr[...], wu_ref[...])
        h = halo[j]
        if step % 8 == 0:
            u1 = jnp.concatenate([h[step:], u[:tm - step]], axis=0)
            u2 = jnp.concatenate([h, u[:tm - 2 * step]], axis=0)
        else:
            assert step == 1
            row = lax.broadcasted_iota(jnp.int32, u.shape, 0)
            u1 = jnp.where(row == 0, h[1:2, :], pltpu.roll(u, 1, 0))
            u2 = jnp.where(row == 0, h[0:1, :], jnp.where(row == 1, h[1:2, :], pltpu.roll(u, 2, 0)))
        c = b_ref[...] + wd_ref[0:1, :] * u2 + wd_ref[1:2, :] * u1 + wd_ref[2:3, :] * u
        new_h = u[tm - hrows:, :]
        halo[j] = new_h
        s_ref[j] = new_h
        return c

    ca = conv_half(wua_ref, wda_ref, ba_ref, halo_a, sa_ref)
    cb = conv_half(wub_ref, wdb_ref, bb_ref, halo_b, sb_ref)
    contrib = _dot((_silu(ca) * cb).astype(BF), wdn_ref[...])

    @pl.when(j == 0)
    def _():
        acc_scr[...] = x_ref[...] + contrib

    @pl.when(j > 0)
    def _():
        acc_scr[...] = acc_scr[...] + contrib

    @pl.when(j == pl.num_programs(2) - 1)
    def _():
        if final_norm:
            o_ref[...] = _rms(acc_scr[...], gf_ref[...])
        else:
            o_ref[...] = acc_scr[...]


def conv_ffn(x, g, hist, w_up, w_dw, b_dw, w_down, n_outer, tm, step, cw, final_g=None):
    m, d = x.shape
    d_ff = w_down.shape[0]
    rows = m // n_outer
    nt = rows // tm
    nj = d_ff // cw
    hrows = (FFN_WIDTH - 1) * step
    assert tm >= hrows and d_ff % cw == 0
    final_norm = final_g is not None

    in_specs = [
        pl.BlockSpec((tm, d), lambda o, t, j: (o * nt + t, 0)),
        pl.BlockSpec((1, d), lambda o, t, j: (0, 0)),
        pl.BlockSpec((None, hrows, cw), lambda o, t, j: (o, 0, j)),
        pl.BlockSpec((None, hrows, cw), lambda o, t, j: (o, 0, nj + j)),
        pl.BlockSpec((d, cw), lambda o, t, j: (0, j)),
        pl.BlockSpec((d, cw), lambda o, t, j: (0, nj + j)),
        pl.BlockSpec((FFN_WIDTH, cw), lambda o, t, j: (0, j)),
        pl.BlockSpec((FFN_WIDTH, cw), lambda o, t, j: (0, nj + j)),
        pl.BlockSpec((1, cw), lambda o, t, j: (0, j)),
        pl.BlockSpec((1, cw), lambda o, t, j: (0, nj + j)),
        pl.BlockSpec((cw, d), lambda o, t, j: (j, 0)),
    ]
    args = [x, g.reshape(1, d), hist, hist, w_up, w_up, w_dw, w_dw, b_dw.reshape(1, -1), b_dw.reshape(1, -1), w_down]
    if final_norm:
        in_specs.append(pl.BlockSpec((1, d), lambda o, t, j: (0, 0)))
        args.append(final_g.reshape(1, d))
    out, sa, sb = pl.pallas_call(
        functools.partial(_ffn_kernel, tm=tm, step=step, final_norm=final_norm),
        grid=(n_outer, nt, nj),
        in_specs=in_specs,
        out_specs=[pl.BlockSpec((tm, d), lambda o, t, j: (o * nt + t, 0)),
                   pl.BlockSpec((None, nj, hrows, cw), lambda o, t, j: (o, 0, 0, 0)),
                   pl.BlockSpec((None, nj, hrows, cw), lambda o, t, j: (o, 0, 0, 0))],
        out_shape=[jax.ShapeDtypeStruct((m, d), F32),
                   jax.ShapeDtypeStruct((n_outer, nj, hrows, cw), F32),
                   jax.ShapeDtypeStruct((n_outer, nj, hrows, cw), F32)],
        scratch_shapes=[pltpu.VMEM((tm, d), BF), pltpu.VMEM((tm, d), F32),
                        pltpu.VMEM((nj, hrows, cw), F32), pltpu.VMEM((nj, hrows, cw), F32)],
        compiler_params=_cparams("parallel", "arbitrary", "arbitrary"),
        name="conv_ffn",
    )(*args)
    def cols(z):
        return z.transpose(0, 2, 1, 3).reshape(n_outer, hrows, d_ff)

    return out, jnp.concatenate([cols(sa), cols(sb)], axis=-1)


def _compress_kernel(p_ref, pe_ref, w1_ref, w2_ref, o_ref, *, n):
    row = lax.broadcasted_iota(jnp.int32, (n, NSA_DH), 0)
    for g in range(NSA_GROUPS):
        x = p_ref[g]
        a = _dot((x + pe_ref[0:1, :]).astype(BF), w1_ref[0])
        b = _dot((x + pe_ref[1:2, :]).astype(BF), w1_ref[1])
        h = _silu(a + pltpu.roll(b, n - 1, 0))
        o = _dot(h.astype(BF), w2_ref[...])
        o_ref[g] = jnp.where(row < n - 1, o, 0.0)


def compress(pieces, pe_ab, w1_ab, w2):
    bsz, _, _, n, pw = pieces.shape
    return pl.pallas_call(
        functools.partial(_compress_kernel, n=n),
        grid=(bsz, 2),
        in_specs=[pl.BlockSpec((None, None, NSA_GROUPS, n, pw), lambda b, c: (b, c, 0, 0, 0)),
                  pl.BlockSpec((None, 2, pw), lambda b, c: (c, 0, 0)),
                  pl.BlockSpec((None, 2, pw, CMP_HIDDEN), lambda b, c: (c, 0, 0, 0)),
                  pl.BlockSpec((None, CMP_HIDDEN, NSA_DH), lambda b, c: (c, 0, 0))],
        out_specs=pl.BlockSpec((None, None, NSA_GROUPS, n, NSA_DH), lambda b, c: (b, c, 0, 0, 0)),
        out_shape=jax.ShapeDtypeStruct((bsz, 2, NSA_GROUPS, n, NSA_DH), F32),
        compiler_params=_cparams("parallel", "parallel"),
        name="nsa_compress",
    )(pieces, pe_ab, w1_ab, w2)


def _bias_softmax(s, bias):
    s = s + bias
    m = jnp.max(s, axis=-1, keepdims=True)
    p = jnp.exp(s - m)
    inv = jnp.where(m > 0.5 * NEG, 1.0 / jnp.sum(p, axis=-1, keepdims=True), 0.0)
    return p * inv


def _stack_heads(x):
    return jnp.concatenate([x] * NSA_REP, axis=0)


def _sum_heads(p, tq):
    out = p[0:tq]
    for r in range(1, NSA_REP):
        out = out + p[r * tq:(r + 1) * tq]
    return out


def _sel_scores(imp, blk, tp, ns):
    cur = tp // SEL_BLOCK
    forced = (blk == 0) | (blk == cur) | (blk == cur - 1)
    valid = blk * SEL_BLOCK <= tp
    score = jnp.where(forced, 1e6, jnp.where(valid, imp, -1e6))
    return jnp.where(blk < ns, score, -3e6)


def _top_n_mask(score, blk, axis, n):
    sel = jnp.zeros(score.shape, F32)
    for _ in range(n):
        mx = jnp.max(score, axis=axis, keepdims=True)
        first = jnp.min(jnp.where(score == mx, blk, LANES), axis=axis, keepdims=True)
        pick = blk == first
        sel = jnp.where(pick, 1.0, sel)
        score = jnp.where(pick, -3e38, score)
    return sel


def _gate_cols(gt, col0, br):
    return jnp.concatenate([gt[:, col0 + r * 3 + br:col0 + r * 3 + br + 1] for r in range(NSA_REP)], axis=0)


def _softmax_keys_on_rows(s):
    m = jnp.max(s, axis=0, keepdims=True)
    p = jnp.exp(s - m)
    inv = jnp.where(m > 0.5 * NEG, 1.0 / jnp.sum(p, axis=0, keepdims=True), 0.0)
    return p * inv


def _nsa_prompt_kernel(q_ref, gt_ref, kc_ref, vct_ref, ks_ref, vst_ref, kw_ref, vwt_ref, o_ref, bsel_scr,
                       *, tq, ncp, ns, ck, sub, nw):
    rep, dh = NSA_REP, NSA_DH
    qb = pl.program_id(2) * tq
    n_chunks = (qb + tq - 1) // ck + 1
    ws = pl.multiple_of(jnp.maximum(qb - WINDOW, 0), tq)
    bpc = ck // LANES

    qt = (q_ref[...] * (dh ** -0.5)).T
    qt_cat = jnp.concatenate([qt[r * dh:(r + 1) * dh] for r in range(rep)], axis=1).astype(BF)

    def head(x, r):
        return x[:, r * tq:(r + 1) * tq]

    def tpos(shape):
        return lax.broadcasted_iota(jnp.int32, shape, 1) + qb

    n_idx = lax.broadcasted_iota(jnp.int32, (ncp, tq), 0)
    bias_c = jnp.where(CMP_STRIDE * n_idx + CMP_LEN - 1 <= tpos((ncp, tq)), 0.0, NEG)
    s_c = _dot(kc_ref[...], qt_cat)
    vct = vct_ref[...]
    o_c, psum = [], None
    for r in range(rep):
        p = _softmax_keys_on_rows(head(s_c, r) + bias_c)
        o_c.append(_dot(vct, p.astype(BF)))
        psum = p if psum is None else psum + p

    wpos = lax.broadcasted_iota(jnp.int32, (nw, tq), 0) + ws
    tw = tpos((nw, tq))
    bias_w = jnp.where((wpos <= tw) & (wpos > tw - WINDOW), 0.0, NEG)
    s_w = _dot(kw_ref[pl.ds(ws, nw), :], qt_cat)
    wblk = ws // LANES
    o_w = []
    for r in range(rep):
        p = _softmax_keys_on_rows(head(s_w, r) + bias_w).astype(BF)
        acc = _dot(vwt_ref[wblk], p[0:LANES])
        for jb in range(1, nw // LANES):
            acc = acc + _dot(vwt_ref[wblk + jb], p[jb * LANES:(jb + 1) * LANES])
        o_w.append(acc)

    cj = lax.broadcasted_iota(jnp.int32, (LANES, ncp), 0) * SEL_BLOCK
    cn = lax.broadcasted_iota(jnp.int32, (LANES, ncp), 1) * CMP_STRIDE
    cover_t = jnp.where((cn <= cj + SEL_BLOCK - 1) & (cn + CMP_LEN - 1 >= cj), 1.0, 0.0).astype(BF)
    hi, mid, lo = _split3(psum)
    imp_t = _dot(cover_t, hi) + _dot(cover_t, mid) + _dot(cover_t, lo)
    blk = lax.broadcasted_iota(jnp.int32, (LANES, tq), 0)
    sel_t = _top_n_mask(_sel_scores(imp_t, blk, tpos((LANES, tq)), ns), blk, 0, min(SEL_TOPN, ns))
    bsel_scr[...] = jnp.where(sel_t > 0.5, 0.0, NEG)

    bps = sub // SEL_BLOCK

    def sel_chunk(c, carry, causal):
        k0 = pl.multiple_of(c * ck, ck)
        brows = bsel_scr[pl.ds(pl.multiple_of(c * (ck // SEL_BLOCK), ck // SEL_BLOCK), ck // SEL_BLOCK), :]
        for h in range(ck // sub):
            s = _dot(ks_ref[pl.ds(k0 + h * sub, sub), :], qt_cat)
            if causal:
                kpos = lax.broadcasted_iota(jnp.int32, (sub, tq), 0) + (k0 + h * sub)
                cbias = jnp.where(kpos <= tpos((sub, tq)), 0.0, NEG)
            out = []
            for r in range(rep):
                m, l, acc = carry[r]
                sr = head(s, r)
                sr = jnp.concatenate(
                    [sr[j * SEL_BLOCK:(j + 1) * SEL_BLOCK] + brows[h * bps + j:h * bps + j + 1] for j in range(bps)],
                    axis=0)
                if causal:
                    sr = sr + cbias
                m_new = jnp.maximum(m, jnp.max(sr, axis=0, keepdims=True))
                alpha = jnp.exp(m - m_new)
                p = jnp.exp(sr - m_new)
                l = alpha * l + jnp.sum(p, axis=0, keepdims=True)
                p = p.astype(BF)
                acc = alpha * acc
                for jb in range(sub // LANES):
                    acc = acc + _dot(vst_ref[c * bpc + h * (sub // LANES) + jb], p[jb * LANES:(jb + 1) * LANES])
                out.append((m_new, l, acc))
            carry = tuple(out)
        return carry

    init = tuple((jnp.full((1, tq), NEG, F32), jnp.zeros((1, tq), F32), jnp.zeros((dh, tq), F32))
                 for _ in range(rep))
    fin = lax.fori_loop(0, n_chunks - 1, functools.partial(sel_chunk, causal=False), init)
    fin = sel_chunk(n_chunks - 1, fin, True)

    gt = _sigmoid(gt_ref[...])
    o_t = []
    for r in range(rep):
        _, l_s, acc_s = fin[r]
        o_s = acc_s / jnp.maximum(l_s, 1e-30)
        o_t.append(gt[3 * r:3 * r + 1] * o_c[r] + gt[3 * r + 1:3 * r + 2] * o_s + gt[3 * r + 2:3 * r + 3] * o_w[r])
    o_ref[...] = jnp.concatenate(o_t, axis=0).T


def nsa_attn_prompt(q, gates_t, kc, vct, ks, vst, kw, vwt, *, tq, ck, sub):
    bsz, t_len, _ = q.shape
    ncp = kc.shape[2]
    gw = NSA_REP * NSA_DH
    ns = -(-t_len // SEL_BLOCK)
    assert tq == LANES and ck % sub == 0 and sub % LANES == 0 and (ck // SEL_BLOCK) % 8 == 0 and ck % tq == 0
    assert ns <= LANES and t_len % ck == 0 and t_len >= WINDOW + tq

    k_spec = pl.BlockSpec((None, None, t_len, NSA_DH), lambda b, g, i: (b, g, 0, 0))
    vt_spec = pl.BlockSpec((None, None, t_len // LANES, NSA_DH, LANES), lambda b, g, i: (b, g, 0, 0, 0))
    return pl.pallas_call(
        functools.partial(_nsa_prompt_kernel, tq=tq, ncp=ncp, ns=ns, ck=ck, sub=sub, nw=WINDOW + tq),
        scratch_shapes=[pltpu.VMEM((LANES, tq), F32)],
        grid=(bsz, NSA_GROUPS, t_len // tq),
        in_specs=[pl.BlockSpec((None, tq, gw), lambda b, g, i: (b, i, g)),
                  pl.BlockSpec((None, None, NSA_REP * 3, tq), lambda b, g, i: (b, g, 0, i)),
                  pl.BlockSpec((None, None, ncp, NSA_DH), lambda b, g, i: (b, g, 0, 0)),
                  pl.BlockSpec((None, None, NSA_DH, ncp), lambda b, g, i: (b, g, 0, 0)),
                  k_spec, vt_spec, k_spec, vt_spec],
        out_specs=pl.BlockSpec((None, tq, gw), lambda b, g, i: (b, i, g)),
        out_shape=jax.ShapeDtypeStruct((bsz, t_len, NSA_GROUPS * gw), F32),
        compiler_params=_cparams("parallel", "parallel", "arbitrary"),
        name="nsa_attn_prompt",
    )(q, gates_t, kc, vct, ks, vst, kw, vwt)


def _nsa_sample_kernel(pt_ref, q_ref, gt_ref, kvn_ref, wn_ref, *rest, n_pages, tq, past_len, wb):
    del pt_ref
    cmp_refs, sel_refs = rest[:n_pages], rest[n_pages:2 * n_pages]
    win_ref, pe_ref, w1_ref, w2_ref, o_ref = rest[2 * n_pages:]
    rep, dh, ng = NSA_REP, NSA_DH, NSA_GROUPS
    rows = rep * tq
    seg = PAGE_SIZE
    ppp = PAGE_SIZE // CMP_STRIDE
    ncp = n_pages * ppp
    nk = past_len + seg
    ns = -(-(past_len + tq) // SEL_BLOCK)

    def tpos(shape):
        return (lax.broadcasted_iota(jnp.int32, shape, 0) & (tq - 1)) + past_len

    kcvc = []
    for c in range(2):
        pieces = jnp.concatenate([cmp_refs[p][c, g] for g in range(ng) for p in range(n_pages)], axis=0)
        a = _dot((pieces + pe_ref[c, 0:1, :]).astype(BF), w1_ref[c, 0])
        b = _dot((pieces + pe_ref[c, 1:2, :]).astype(BF), w1_ref[c, 1])
        per_group = []
        for g in range(ng):
            h = _silu(a[g * ncp:(g + 1) * ncp] + pltpu.roll(b[g * ncp:(g + 1) * ncp], ncp - 1, 0))
            per_group.append(_dot(h.astype(BF), w2_ref[c]).astype(BF))
        kcvc.append(per_group)

    q = q_ref[...] * (dh ** -0.5)
    kvn = kvn_ref[...]
    wn = wn_ref[...]
    kvw = ng * dh

    def q4_of(g):
        return jnp.concatenate([q[:, (g * rep + r) * dh:(g * rep + r + 1) * dh] for r in range(rep)],
                               axis=0).astype(BF)

    def new_seg(x, col):
        return jnp.concatenate([x[:, col:col + dh], jnp.zeros((seg - tq, dh), F32)], axis=0).astype(BF)

    n_idx = lax.broadcasted_iota(jnp.int32, (rows, ncp), 1)
    bias_c = jnp.where(CMP_STRIDE * n_idx + CMP_LEN - 1 <= tpos((rows, ncp)), 0.0, NEG)
    q4s, o_cs, psums = [], [], []
    for g in range(ng):
        q4 = q4_of(g)
        p_c = _bias_softmax(_dot_nt(q4, kcvc[0][g]), bias_c)
        q4s.append(q4)
        o_cs.append(_dot(p_c.astype(BF), kcvc[1][g]))
        psums.append(_sum_heads(p_c, tq))
    cn = lax.broadcasted_iota(jnp.int32, (ncp, LANES), 0) * CMP_STRIDE
    cj = lax.broadcasted_iota(jnp.int32, (ncp, LANES), 1) * SEL_BLOCK
    cover = jnp.where((cn <= cj + SEL_BLOCK - 1) & (cn + CMP_LEN - 1 >= cj), 1.0, 0.0).astype(BF)
    hi, mid, lo = _split3(jnp.concatenate(psums, axis=0))
    imp = _dot(hi, cover) + _dot(mid, cover) + _dot(lo, cover)
    blk = lax.broadcasted_iota(jnp.int32, (ng * tq, LANES), 1)
    sel = _top_n_mask(_sel_scores(imp, blk, tpos((ng * tq, LANES)), ns), blk, 1, min(SEL_TOPN, ns))
    ej = lax.broadcasted_iota(jnp.int32, (LANES, nk), 0)
    ek = lax.broadcasted_iota(jnp.int32, (LANES, nk), 1)
    expand = jnp.where(ek // SEL_BLOCK == ej, 1.0, 0.0).astype(BF)
    sel_rows = jnp.concatenate([sel[g * tq:(g + 1) * tq] for g in range(ng) for _ in range(rep)], axis=0)
    mk_all = _dot(sel_rows.astype(BF), expand)
    kpos = lax.broadcasted_iota(jnp.int32, (rows, nk), 1)
    causal = kpos <= tpos((rows, nk))
    wp = lax.broadcasted_iota(jnp.int32, (rows, wb + seg), 1) + (past_len - wb)
    tw = tpos((rows, wb + seg))
    bias_w = jnp.where((wp <= tw) & (wp > tw - WINDOW), 0.0, NEG)

    gt = _sigmoid(gt_ref[...])
    for g in range(ng):
        q4 = q4s[g]
        sc = jnp.concatenate([_dot(q4, sel_refs[p][0, g]) for p in range(n_pages)]
                             + [_dot_nt(q4, new_seg(kvn, 2 * kvw + g * dh))], axis=1)
        bias = jnp.where(causal & (mk_all[g * rows:(g + 1) * rows] > 0.5), 0.0, NEG)
        p_s = _bias_softmax(sc, bias).astype(BF)
        o_s = _dot(p_s[:, past_len:], new_seg(kvn, 3 * kvw + g * dh))
        for p in range(n_pages):
            o_s = o_s + _dot_nt(p_s[:, p * seg:(p + 1) * seg], sel_refs[p][1, g])
        sw = jnp.concatenate([_dot(q4, win_ref[0, g]), _dot_nt(q4, new_seg(wn, g * dh))], axis=1)
        p_w = _bias_softmax(sw, bias_w).astype(BF)
        o_w = _dot_nt(p_w[:, :wb], win_ref[1, g]) + _dot(p_w[:, wb:], new_seg(wn, kvw + g * dh))
        c0 = g * rep * 3
        o4 = _gate_cols(gt, c0, 0) * o_cs[g] + _gate_cols(gt, c0, 1) * o_s + _gate_cols(gt, c0, 2) * o_w
        o_ref[:, g * rep * dh:(g + 1) * rep * dh] = jnp.concatenate(
            [o4[r * tq:(r + 1) * tq] for r in range(rep)], axis=1)


def nsa_attn_sample(q, gates, kv_new, win_new, cmp_pages, sel_pages, win_t, page_table, pe_ab, w1_ab, w2):
    bsz, tq, hq = q.shape
    n_pages = page_table.shape[1]
    wb = win_t.shape[-1]
    assert tq & (tq - 1) == 0 and tq <= PAGE_SIZE

    def row_spec(w):
        return pl.BlockSpec((None, tq, w), lambda b, pt: (b, 0, 0))

    def page_spec(shape, p):
        return pl.BlockSpec((None,) + shape, lambda b, pt: (pt[b, p],) + (0,) * len(shape))

    def full_spec(x):
        return pl.BlockSpec(x.shape, lambda b, pt: (0,) * x.ndim)

    in_specs = ([row_spec(hq), row_spec(gates.shape[2]), row_spec(kv_new.shape[2]), row_spec(win_new.shape[2])]
                + [page_spec(cmp_pages.shape[1:], p) for p in range(n_pages)]
                + [page_spec(sel_pages.shape[1:], p) for p in range(n_pages)]
                + [pl.BlockSpec((None,) + win_t.shape[1:], lambda b, pt: (b, 0, 0, 0, 0)),
                   full_spec(pe_ab), full_spec(w1_ab), full_spec(w2)])
    return pl.pallas_call(
        functools.partial(_nsa_sample_kernel, n_pages=n_pages, tq=tq, past_len=n_pages * PAGE_SIZE, wb=wb),
        grid_spec=pltpu.PrefetchScalarGridSpec(
            num_scalar_prefetch=1, grid=(bsz,), in_specs=in_specs, out_specs=row_spec(hq)),
        out_shape=jax.ShapeDtypeStruct((bsz, tq, hq), F32),
        compiler_params=_cparams("parallel"),
        name="nsa_attn_sample",
    )(page_table, q, gates, kv_new, win_new, *([cmp_pages] * n_pages), *([sel_pages] * n_pages),
      win_t, pe_ab, w1_ab, w2)


def _pad_cols(w, n):
    return jnp.pad(w, ((0, 0), (0, n - w.shape[1])))


def _group_major(x, bsz, t_len):
    return x.reshape(bsz, t_len, NSA_GROUPS, NSA_DH).transpose(0, 2, 1, 3)


def _group_major_t(x, bsz, t_len):
    x5 = x.reshape(bsz, t_len // LANES, LANES, NSA_GROUPS, NSA_DH)
    return x5.transpose(0, 3, 1, 4, 2)


def _to_time_major(x, bsz, t_len):
    return x.reshape(bsz, t_len, -1).transpose(1, 0, 2).reshape(t_len * bsz, -1)


def _to_batch_major(x, bsz, t_len):
    return x.reshape(t_len, bsz, -1).transpose(1, 0, 2).reshape(bsz * t_len, -1)


def kernel(x_prompt, x_sample, cache_gla_state, cache_conv, cache_nsa_kv, cache_nsa_win, cache_mem_kv,
           cache_ffn_conv, page_table, mem_prompt, norm_mix, norm_mem, norm_x, norm_ffn, norm_final,
           w_in_a, w_gate_a, b_gate_a, g_gla_out, w_dw_b, b_dw_b, g_ln_b, b_ln_b, w_out_a,
           w_in_c, pe_cmp, w_cmp1, w_cmp2, w_out_c, w_xq, w_mem_kv, w_xo, w_up, w_ffn_dw, b_ffn_dw, w_down):
    bp, tp, d = x_prompt.shape
    bs, ts, _ = x_sample.shape
    depth = norm_mix.shape[0]
    n_mem = mem_prompt.shape[1]
    d_ff = w_down.shape[1]
    hk, hv = GLA_HEADS * GLA_DK, GLA_HEADS * GLA_DV
    kvw = NSA_GROUPS * NSA_DH
    hq = NSA_HEADS * NSA_DH

    xp = x_prompt.reshape(bp * tp, d)
    xs = x_sample.reshape(bs * ts, d)
    tm_p = 512
    tm_s = bs * ts

    outs = {k: [] for k in ("gla_p", "gla_s", "conv_p", "conv_s", "nsa_p", "nsa_s", "win_p", "win_s",
                            "mem_p", "ffn_p", "ffn_s")}
    for l in range(depth):
        i = l // 2
        if l % 2 == 0:
            wi = w_in_a[i]
            c0, c1, c2, c3, c4 = hk, 2 * hk, 2 * hk + hv, 2 * hk + 2 * hv, 2 * hk + 2 * hv + GLA_RANK
            w_a = jnp.concatenate([wi[:, c4:], wi[:, c1:c2], wi[:, c2:c3], wi[:, :c0], wi[:, c0:c1],
                                   _pad_cols(wi[:, c3:c4], LANES)], axis=1).astype(BF)
            splits = (2 * CONV_CH, hv, hv, hk, hk, LANES)
            wg = jnp.pad(w_gate_a[i], ((0, LANES - GLA_RANK), (0, 0))).astype(BF)
            bg = b_gate_a[i].reshape(1, hk)
            go = g_gla_out[i].reshape(1, hv)
            w_out = w_out_a[i].astype(BF)
            vecs = [v.reshape(1, CONV_CH) for v in (b_dw_b[i], g_ln_b[i], b_ln_b[i])]

            def mixer(x, bsz, t_len, tm, s0, hist_tm, n_outer, ctm, step):
                u, v, r, q, k, a = norm_matmul(x, norm_mix[l], w_a, splits, tm)
                sh = lambda z: z.reshape(bsz, t_len, -1)
                og, s_new = gla(sh(q), sh(k), sh(v), sh(r), sh(a), wg, bg, go, s0, 1024)
                if step > 1:
                    u = _to_time_major(u, bsz, t_len)
                c, hist_new = convmod(u, hist_tm, w_dw_b[i], *vecs, n_outer, ctm, step)
                if step > 1:
                    c = _to_batch_major(c, bsz, t_len)
                y = matmul_res([og.reshape(bsz * t_len, hv), c], [w_out[:hv], w_out[hv:]], x, tm)
                return y, s_new, hist_new

            xp, sp, cp = mixer(xp, bp, tp, tm_p, jnp.zeros((bp, GLA_HEADS, GLA_DK, GLA_DV), F32),
                               jnp.zeros((bp, CONV_WIDTH - 1, CONV_CH), F32), bp, 512, 1)
            hist_s = cache_conv[i].transpose(1, 0, 2).reshape(1, (CONV_WIDTH - 1) * bs, CONV_CH)
            xs, ss, cs = mixer(xs, bs, ts, tm_s, cache_gla_state[i], hist_s, 1, bs * ts, bs)
            cs = cs.reshape(CONV_WIDTH - 1, bs, CONV_CH).transpose(1, 0, 2)
            outs["gla_p"].append(sp)
            outs["gla_s"].append(ss)
            outs["conv_p"].append(cp)
            outs["conv_s"].append(cs)
        else:
            n_in = hq + 6 * kvw + 3 * NSA_HEADS
            n_pad = -(-n_in // LANES) * LANES
            w_c = _pad_cols(w_in_c[i], n_pad).astype(BF)
            splits = (hq, 4 * kvw, 2 * kvw, n_pad - hq - 6 * kvw)
            w_out = w_out_c[i].astype(BF)
            pw = CMP_STRIDE * NSA_DH
            pe_ab = pe_cmp[i].reshape(2, 2, pw)
            w1_ab = w_cmp1[i].reshape(2, 2, pw, CMP_HIDDEN).astype(BF)
            w2 = w_cmp2[i].astype(BF)

            def pieces_of(rows, bsz, n):
                r6 = rows[:, :n * CMP_STRIDE].reshape(bsz, n, CMP_STRIDE, 2, NSA_GROUPS, NSA_DH)
                return r6.transpose(0, 3, 4, 1, 2, 5).reshape(bsz, 2, NSA_GROUPS, n, pw)

            q, kv4, win, gl = norm_matmul(xp, norm_mix[l], w_c, splits, tm_p)
            kv4b = kv4.reshape(bp, tp, 4 * kvw)
            kcvc = compress(pieces_of(kv4b[:, :, :2 * kvw], bp, tp // CMP_STRIDE), pe_ab, w1_ab, w2)
            ks = _group_major(kv4[:, 2 * kvw:3 * kvw], bp, tp).astype(BF)
            vst = _group_major_t(kv4[:, 3 * kvw:], bp, tp).astype(BF)
            kw = _group_major(win[:, :kvw], bp, tp).astype(BF)
            vwt = _group_major_t(win[:, kvw:], bp, tp).astype(BF)
            gates_t = gl[:, :3 * NSA_HEADS].reshape(bp, tp, NSA_GROUPS, NSA_REP * 3).transpose(0, 2, 3, 1)
            o = nsa_attn_prompt(q.reshape(bp, tp, hq), gates_t, kcvc[:, 0].astype(BF),
                                kcvc[:, 1].transpose(0, 1, 3, 2).astype(BF), ks, vst, kw, vwt,
                                tq=Q_BLOCK, ck=512, sub=512)
            xp = matmul_res([o.reshape(bp * tp, hq)], [w_out], xp, tm_p)
            win_rows = min(WINDOW, tp)
            outs["nsa_p"].append(kv4.reshape(bp, tp, 4, NSA_GROUPS, NSA_DH))
            outs["win_p"].append(win.reshape(bp, tp, 2, NSA_GROUPS, NSA_DH)[:, tp - win_rows:])

            q, kv4, win, gl = norm_matmul(xs, norm_mix[l], w_c, splits, tm_s)
            pool = cache_nsa_kv[i]
            n_phys = pool.shape[0]
            ppp = PAGE_SIZE // CMP_STRIDE
            cmp_pages = (pool[:, :, :2].reshape(n_phys, ppp, CMP_STRIDE, 2, NSA_GROUPS, NSA_DH)
                         .transpose(0, 3, 4, 1, 2, 5).reshape(n_phys, 2, NSA_GROUPS, ppp, pw))
            sel_pages = pool[:, :, 2:].transpose(0, 2, 3, 4, 1).astype(BF)
            win_cache = cache_nsa_win[i]
            wb = win_cache.shape[1]
            win_t = win_cache.transpose(0, 2, 3, 4, 1).astype(BF)
            o = nsa_attn_sample(q.reshape(bs, ts, hq), gl.reshape(bs, ts, -1), kv4.reshape(bs, ts, 4 * kvw),
                                win.reshape(bs, ts, 2 * kvw), cmp_pages, sel_pages, win_t, page_table,
                                pe_ab, w1_ab, w2)
            xs = matmul_res([o.reshape(bs * ts, hq)], [w_out], xs, tm_s)
            outs["nsa_s"].append(kv4.reshape(bs, ts, 4, NSA_GROUPS, NSA_DH))
            win_new = win.reshape(bs, ts, 2, NSA_GROUPS, NSA_DH)
            outs["win_s"].append(jnp.concatenate([win_cache, win_new], axis=1)[:, ts:])

        xhw = X_HEADS * X_DH
        (mkv,) = norm_matmul(mem_prompt.reshape(bp * n_mem, d), norm_mem[l], w_mem_kv[l].astype(BF),
                             (2 * xhw,), bp * n_mem)
        outs["mem_p"].append(mkv.reshape(bp, n_mem, 2, X_HEADS, X_DH))
        w_q = w_xq[l].astype(BF)
        w_o = w_xo[l].astype(BF)
        (qx,) = norm_matmul(xp, norm_x[l], w_q, (xhw,), tm_p)
        ox = xattn(qx.reshape(bp, tp, xhw), mkv.reshape(bp, n_mem, 2 * xhw), 512)
        xp = matmul_res([ox.reshape(bp * tp, xhw)], [w_o], xp, tm_p)
        (qx,) = norm_matmul(xs, norm_x[l], w_q, (xhw,), tm_s)
        ox = xattn(qx.reshape(bs, ts, xhw), cache_mem_kv[l].reshape(bs, n_mem, 2 * xhw), ts)
        xs = matmul_res([ox.reshape(bs * ts, xhw)], [w_o], xs, tm_s)

        wu = w_up[l].astype(BF)
        wd = w_down[l].astype(BF)
        fg = norm_final if l == depth - 1 else None
        xp, hfp = conv_ffn(xp, norm_ffn[l], jnp.zeros((bp, FFN_WIDTH - 1, 2 * d_ff), F32), wu, w_ffn_dw[l],
                           b_ffn_dw[l], wd, bp, 512, 1, 256, fg)
        hist_s = cache_ffn_conv[l].transpose(1, 0, 2).reshape(1, (FFN_WIDTH - 1) * bs, 2 * d_ff)
        xs_tm, hfs = conv_ffn(_to_time_major(xs, bs, ts), norm_ffn[l], hist_s, wu, w_ffn_dw[l], b_ffn_dw[l], wd,
                              1, bs * ts, bs, 256, fg)
        xs = _to_batch_major(xs_tm, bs, ts)
        outs["ffn_p"].append(hfp)
        outs["ffn_s"].append(hfs.reshape(FFN_WIDTH - 1, bs, 2 * d_ff).transpose(1, 0, 2))

    st = jnp.stack
    return (xp.reshape(bp, tp, d), xs.reshape(bs, ts, d),
            st(outs["gla_p"]), st(outs["gla_s"]), st(outs["conv_p"]), st(outs["conv_s"]),
            st(outs["nsa_p"]), st(outs["nsa_s"]), st(outs["win_p"]), st(outs["win_s"]),
            st(outs["mem_p"]), st(outs["ffn_p"]), st(outs["ffn_s"]))
```

```python
import functools
import math

import jax
import jax.numpy as jnp
from jax import lax
from jax.experimental import pallas as pl
from jax.experimental.pallas import tpu as pltpu

EPS = 1e-6
NEG = -1e30
F32 = jnp.float32
BF = jnp.bfloat16

V7X_VMEM_BYTES = 64 * 1024 * 1024
VMEM_LIMIT = V7X_VMEM_BYTES - 8 * 1024 * 1024
LANES = 128

GLA_HEADS, GLA_DK, GLA_DV, GLA_RANK, GLA_TAU, GLA_CHUNK = 4, 64, 128, 16, 16.0, 16
GLA_UNROLL = 4
CONV_CH, CONV_WIDTH = 512, 31
NSA_HEADS, NSA_GROUPS, NSA_DH = 16, 4, 64
NSA_REP = NSA_HEADS // NSA_GROUPS
CMP_LEN, CMP_STRIDE, CMP_HIDDEN = 32, 16, 64
SEL_BLOCK, SEL_TOPN, WINDOW, Q_BLOCK = 64, 16, 512, 128
X_HEADS, X_DH = 4, 128
FFN_WIDTH = 3
PAGE_SIZE = 128


def _cparams(*sem):
    return pltpu.CompilerParams(dimension_semantics=sem, vmem_limit_bytes=VMEM_LIMIT)


def _rms(x, g):
    return x * lax.rsqrt(jnp.mean(x * x, axis=-1, keepdims=True) + EPS) * g


def _sigmoid(x):
    return 1.0 / (1.0 + jnp.exp(-x))


def _silu(x):
    return x * _sigmoid(x)


def _dot(a, b):
    return jnp.dot(a, b, preferred_element_type=F32)


def _dot_nt(a, b):
    return lax.dot_general(a, b, (((1,), (1,)), ((), ())), preferred_element_type=F32)


def _dot_tn(a, b):
    return lax.dot_general(a, b, (((0,), (0,)), ((), ())), preferred_element_type=F32)


def _split3(x):
    hi = x.astype(BF)
    r1 = x - hi.astype(F32)
    mid = r1.astype(BF)
    lo = (r1 - mid.astype(F32)).astype(BF)
    return hi, mid, lo


def _norm_matmul_kernel(x_ref, g_ref, w_ref, *o_refs, splits):
    xn = _rms(x_ref[...], g_ref[...]).astype(BF)
    off = 0
    for o_ref, n in zip(o_refs, splits):
        o_ref[...] = _dot(xn, w_ref[:, off:off + n])
        off += n


def norm_matmul(x, g, w, splits, tm):
    m, d = x.shape
    n = w.shape[1]
    assert sum(splits) == n and m % tm == 0
    outs = pl.pallas_call(
        functools.partial(_norm_matmul_kernel, splits=tuple(splits)),
        grid=(m // tm,),
        in_specs=[pl.BlockSpec((tm, d), lambda i: (i, 0)),
                  pl.BlockSpec((1, d), lambda i: (0, 0)),
                  pl.BlockSpec((d, n), lambda i: (0, 0))],
        out_specs=[pl.BlockSpec((tm, s), lambda i: (i, 0)) for s in splits],
        out_shape=[jax.ShapeDtypeStruct((m, s), F32) for s in splits],
        compiler_params=_cparams("parallel"),
        name="norm_matmul",
    )(x, g.reshape(1, d), w)
    return outs


def _matmul_res_kernel(*refs, n_in):
    a_refs, w_refs = refs[:n_in], refs[n_in:2 * n_in]
    res_ref, o_ref = refs[2 * n_in], refs[2 * n_in + 1]
    acc = _dot(a_refs[0][...].astype(BF), w_refs[0][...])
    for a_ref, w_ref in zip(a_refs[1:], w_refs[1:]):
        acc = acc + _dot(a_ref[...].astype(BF), w_ref[...])
    o_ref[...] = res_ref[...] + acc


def matmul_res(a_list, w_list, res, tm):
    m, d = res.shape
    n_in = len(a_list)
    return pl.pallas_call(
        functools.partial(_matmul_res_kernel, n_in=n_in),
        grid=(m // tm,),
        in_specs=([pl.BlockSpec((tm, a.shape[1]), lambda i: (i, 0)) for a in a_list]
                  + [pl.BlockSpec(w.shape, lambda i: (0, 0)) for w in w_list]
                  + [pl.BlockSpec((tm, d), lambda i: (i, 0))]),
        out_specs=pl.BlockSpec((tm, d), lambda i: (i, 0)),
        out_shape=jax.ShapeDtypeStruct((m, d), F32),
        compiler_params=_cparams("parallel"),
        name="matmul_res",
    )(*a_list, *w_list, res)


def _gla_kernel(q_ref, k_ref, v_ref, r_ref, a_ref, wg_ref, bg_ref, go_ref, s0_ref,
                o_ref, sout_ref, s_scr, b_scr, *, chunk, n_chunks):
    j = pl.program_id(1)
    tt = chunk * n_chunks
    blk = min(LANES, tt)

    @pl.when(j == 0)
    def _():
        s_scr[...] = s0_ref[...]

    ri = lax.broadcasted_iota(jnp.int32, (chunk, chunk), 0)
    ci = lax.broadcasted_iota(jnp.int32, (chunk, chunk), 1)
    tri = (ri >= ci).astype(F32)
    eye = (lax.broadcasted_iota(jnp.int32, (GLA_DK, GLA_DK), 0)
           == lax.broadcasted_iota(jnp.int32, (GLA_DK, GLA_DK), 1)).astype(F32)

    rb = lax.broadcasted_iota(jnp.int32, (blk, blk), 0)
    cb = lax.broadcasted_iota(jnp.int32, (blk, blk), 1)
    tri_blk = jnp.where((rb >= cb) & (rb // chunk == cb // chunk), 1.0, 0.0).astype(BF)
    for nb in range(tt // blk):
        rs = slice(nb * blk, (nb + 1) * blk)
        z = _dot(a_ref[rs, :].astype(BF), wg_ref[...]) + bg_ref[...]
        log_a = (jnp.minimum(z, 0.0) - jnp.log(1.0 + jnp.exp(-jnp.abs(z)))) / GLA_TAU
        hi, mid, lo = _split3(log_a)
        b_scr[rs, :] = _dot(tri_blk, hi) + _dot(tri_blk, mid) + _dot(tri_blk, lo)

    def body(i, carry):
        sl = pl.ds(pl.multiple_of(i * chunk, chunk), chunk)
        qc = q_ref[sl, :] * (GLA_DK ** -0.5)
        kc = k_ref[sl, :]
        b = b_scr[sl, :]
        new_s = []
        for h in range(GLA_HEADS):
            ks = slice(h * GLA_DK, (h + 1) * GLA_DK)
            vs = slice(h * GLA_DV, (h + 1) * GLA_DV)
            bh = b[:, ks]
            b_last = bh[chunk - 1:chunk, :]
            qe = (qc[:, ks] * jnp.exp(bh)).astype(BF)
            ke = (kc[:, ks] * jnp.exp(-bh)).astype(BF)
            kl = (kc[:, ks] * jnp.exp(b_last - bh)).astype(BF)
            vh = v_ref[sl, vs].astype(BF)
            s = carry[h]
            att = _dot_nt(qe, ke) * tri
            o = _dot(att.astype(BF), vh) + _dot(qe, s.astype(BF))
            decay = jnp.exp(jnp.sum(eye * b_last, axis=1, keepdims=True))
            new_s.append(decay * s + _dot_tn(kl, vh))
            o = o * lax.rsqrt(jnp.mean(o * o, axis=-1, keepdims=True) + EPS) * go_ref[:, vs]
            o_ref[sl, vs] = o * _silu(r_ref[sl, vs])
        return tuple(new_s)

    s_fin = lax.fori_loop(0, n_chunks, body, tuple(s_scr[h] for h in range(GLA_HEADS)),
                          unroll=math.gcd(n_chunks, GLA_UNROLL))
    for h in range(GLA_HEADS):
        s_scr[h] = s_fin[h]
    sout_ref[...] = s_scr[...]


def gla(q, k, v, r, a, wg, bg, go, s0, tt):
    bsz, t_len, _ = q.shape
    chunk = math.gcd(t_len, GLA_CHUNK)
    tt = min(tt, t_len)
    hk, hv = GLA_HEADS * GLA_DK, GLA_HEADS * GLA_DV

    def tspec(w):
        return pl.BlockSpec((None, tt, w), lambda b, j: (b, j, 0))

    sspec = pl.BlockSpec((None, GLA_HEADS, GLA_DK, GLA_DV), lambda b, j: (b, 0, 0, 0))
    return pl.pallas_call(
        functools.partial(_gla_kernel, chunk=chunk, n_chunks=tt // chunk),
        grid=(bsz, t_len // tt),
        in_specs=[tspec(hk), tspec(hk), tspec(hv), tspec(hv), tspec(LANES),
                  pl.BlockSpec((LANES, hk), lambda b, j: (0, 0)),
                  pl.BlockSpec((1, hk), lambda b, j: (0, 0)),
                  pl.BlockSpec((1, hv), lambda b, j: (0, 0)),
                  sspec],
        out_specs=[tspec(hv), sspec],
        out_shape=[jax.ShapeDtypeStruct((bsz, t_len, hv), F32),
                   jax.ShapeDtypeStruct((bsz, GLA_HEADS, GLA_DK, GLA_DV), F32)],
        scratch_shapes=[pltpu.VMEM((GLA_HEADS, GLA_DK, GLA_DV), F32), pltpu.VMEM((tt, hk), F32)],
        compiler_params=_cparams("parallel", "arbitrary"),
        name="gla",
    )(q, k, v, r, a, wg, bg, go, s0)


def _convmod_kernel(u1_ref, u2_ref, hist_ref, w_ref, b_ref, g_ref, bl_ref, c_ref, hout_ref, gpad,
                    *, tm, step, off):
    hrows = (CONV_WIDTH - 1) * step
    base = off - hrows

    @pl.when(pl.program_id(1) == 0)
    def _():
        gpad[base:off, :] = hist_ref[...]

    gpad[off:off + tm, :] = u1_ref[...] * _sigmoid(u2_ref[...])
    acc = b_ref[...] + w_ref[0:1, :] * gpad[base:base + tm, :]
    for kk in range(1, CONV_WIDTH):
        acc = acc + w_ref[kk:kk + 1, :] * gpad[base + kk * step:base + kk * step + tm, :]
    d = acc - jnp.mean(acc, axis=-1, keepdims=True)
    c = d * lax.rsqrt(jnp.mean(d * d, axis=-1, keepdims=True) + EPS) * g_ref[...] + bl_ref[...]
    c_ref[...] = _silu(c)
    new_hist = gpad[base + tm:off + tm, :]
    hout_ref[...] = new_hist
    gpad[base:off, :] = new_hist


def convmod(u, hist, w, b, g, bl, n_outer, tm, step):
    m = u.shape[0]
    rows = m // n_outer
    nt = rows // tm
    hrows = (CONV_WIDTH - 1) * step
    assert tm >= hrows or nt == 1
    off = -(-hrows // 8) * 8
    ch = CONV_CH
    vec = pl.BlockSpec((1, ch), lambda o, t: (0, 0))
    return pl.pallas_call(
        functools.partial(_convmod_kernel, tm=tm, step=step, off=off),
        grid=(n_outer, nt),
        in_specs=[pl.BlockSpec((tm, ch), lambda o, t: (o * nt + t, 0)),
                  pl.BlockSpec((tm, ch), lambda o, t: (o * nt + t, 1)),
                  pl.BlockSpec((None, hrows, ch), lambda o, t: (o, 0, 0)),
                  pl.BlockSpec((CONV_WIDTH, ch), lambda o, t: (0, 0)),
                  vec, vec, vec],
        out_specs=[pl.BlockSpec((tm, ch), lambda o, t: (o * nt + t, 0)),
                   pl.BlockSpec((None, hrows, ch), lambda o, t: (o, 0, 0))],
        out_shape=[jax.ShapeDtypeStruct((m, ch), F32),
                   jax.ShapeDtypeStruct((n_outer, hrows, ch), F32)],
        scratch_shapes=[pltpu.VMEM((off + tm, ch), F32)],
        compiler_params=_cparams("parallel", "arbitrary"),
        name="convmod",
    )(u, u, hist, w, b, g, bl)


def _xattn_kernel(q_ref, kv_ref, o_ref):
    q = q_ref[...] * (X_DH ** -0.5)
    for h in range(X_HEADS):
        sl = slice(h * X_DH, (h + 1) * X_DH)
        kh = kv_ref[:, 0, h, :].astype(BF)
        vh = kv_ref[:, 1, h, :].astype(BF)
        s = _dot_nt(q[:, sl].astype(BF), kh)
        p = jnp.exp(s - jnp.max(s, axis=-1, keepdims=True))
        p = p / jnp.sum(p, axis=-1, keepdims=True)
        o_ref[:, sl] = _dot(p.astype(BF), vh)


def xattn(q, kv, layer, tq):
    bsz, t_len, hw = q.shape
    n_mem = kv.shape[2]
    return pl.pallas_call(
        _xattn_kernel,
        grid=(bsz, t_len // tq),
        in_specs=[pl.BlockSpec((None, tq, hw), lambda b, i: (b, i, 0)),
                  pl.BlockSpec((None, None, n_mem, 2, X_HEADS, X_DH), lambda b, i: (layer, b, 0, 0, 0, 0))],
        out_specs=pl.BlockSpec((None, tq, hw), lambda b, i: (b, i, 0)),
        out_shape=jax.ShapeDtypeStruct((bsz, t_len, hw), F32),
        compiler_params=_cparams("parallel", "parallel"),
        name="xattn",
    )(q, kv)


def _ffn_kernel(*refs, tm, step, final_norm):
    (x_ref, g_ref, ha_ref, hb_ref, wua_ref, wub_ref, wda_ref, wdb_ref, ba_ref, bb_ref, wdn_ref) = refs[:11]
    rest = refs[11:]
    if final_norm:
        gf_ref, rest = rest[0], rest[1:]
    o_ref, sa_ref, sb_ref, xn_scr, acc_scr, halo_a, halo_b = rest
    t = pl.program_id(1)
    j = pl.program_id(2)
    hrows = (FFN_WIDTH - 1) * step

    @pl.when(j == 0)
    def _():
        xn_scr[...] = _rms(x_ref[...], g_ref[...]).astype(BF)

    @pl.when(t == 0)
    def _():
        halo_a[j] = ha_ref[...]
        halo_b[j] = hb_ref[...]

    def conv_half(wu_ref, wd_ref, b_ref, halo, s_ref):
        u = _dot(xn_scr[...], wu_ref[...])
        h = halo[j]
        if step % 8 == 0:
            u1 = jnp.concatenate([h[step:], u[:tm - step]], axis=0)
            u2 = jnp.concatenate([h, u[:tm - 2 * step]], axis=0)
        else:
            assert step == 1
            row = lax.broadcasted_iota(jnp.int32, (8, u.shape[1]), 0)
            r1, r2 = pltpu.roll(u, 1, 0), pltpu.roll(u, 2, 0)
            top1 = jnp.where(row == 0, h[1:2, :], r1[0:8])
            top2 = jnp.where(row == 0, h[0:1, :], jnp.where(row == 1, h[1:2, :], r2[0:8]))
            u1 = jnp.concatenate([top1, r1[8:]], axis=0)
            u2 = jnp.concatenate([top2, r2[8:]], axis=0)
        c = b_ref[...] + wd_ref[0:1, :] * u2 + wd_ref[1:2, :] * u1 + wd_ref[2:3, :] * u
        new_h = u[tm - hrows:, :]
        halo[j] = new_h
        s_ref[j] = new_h
        return c

    ca = conv_half(wua_ref, wda_ref, ba_ref, halo_a, sa_ref)
    cb = conv_half(wub_ref, wdb_ref, bb_ref, halo_b, sb_ref)
    contrib = _dot((_silu(ca) * cb).astype(BF), wdn_ref[...])

    @pl.when(j == 0)
    def _():
        acc_scr[...] = x_ref[...] + contrib

    @pl.when(j > 0)
    def _():
        acc_scr[...] = acc_scr[...] + contrib

    @pl.when(j == pl.num_programs(2) - 1)
    def _():
        if final_norm:
            o_ref[...] = _rms(acc_scr[...], gf_ref[...])
        else:
            o_ref[...] = acc_scr[...]


def conv_ffn(x, g, hist, w_up, w_dw, b_dw, w_down, n_outer, tm, step, cw, final_g=None):
    m, d = x.shape
    d_ff = w_down.shape[0]
    rows = m // n_outer
    nt = rows // tm
    nj = d_ff // cw
    hrows = (FFN_WIDTH - 1) * step
    assert tm >= hrows and d_ff % cw == 0
    final_norm = final_g is not None

    in_specs = [
        pl.BlockSpec((tm, d), lambda o, t, j: (o * nt + t, 0)),
        pl.BlockSpec((1, d), lambda o, t, j: (0, 0)),
        pl.BlockSpec((None, hrows, cw), lambda o, t, j: (o, 0, j)),
        pl.BlockSpec((None, hrows, cw), lambda o, t, j: (o, 0, nj + j)),
        pl.BlockSpec((d, cw), lambda o, t, j: (0, j)),
        pl.BlockSpec((d, cw), lambda o, t, j: (0, nj + j)),
        pl.BlockSpec((FFN_WIDTH, cw), lambda o, t, j: (0, j)),
        pl.BlockSpec((FFN_WIDTH, cw), lambda o, t, j: (0, nj + j)),
        pl.BlockSpec((1, cw), lambda o, t, j: (0, j)),
        pl.BlockSpec((1, cw), lambda o, t, j: (0, nj + j)),
        pl.BlockSpec((cw, d), lambda o, t, j: (j, 0)),
    ]
    args = [x, g.reshape(1, d), hist, hist, w_up, w_up, w_dw, w_dw, b_dw.reshape(1, -1), b_dw.reshape(1, -1), w_down]
    if final_norm:
        in_specs.append(pl.BlockSpec((1, d), lambda o, t, j: (0, 0)))
        args.append(final_g.reshape(1, d))
    out, sa, sb = pl.pallas_call(
        functools.partial(_ffn_kernel, tm=tm, step=step, final_norm=final_norm),
        grid=(n_outer, nt, nj),
        in_specs=in_specs,
        out_specs=[pl.BlockSpec((tm, d), lambda o, t, j: (o * nt + t, 0)),
                   pl.BlockSpec((None, nj, hrows, cw), lambda o, t, j: (o, 0, 0, 0)),
                   pl.BlockSpec((None, nj, hrows, cw), lambda o, t, j: (o, 0, 0, 0))],
        out_shape=[jax.ShapeDtypeStruct((m, d), F32),
                   jax.ShapeDtypeStruct((n_outer, nj, hrows, cw), F32),
                   jax.ShapeDtypeStruct((n_outer, nj, hrows, cw), F32)],
        scratch_shapes=[pltpu.VMEM((tm, d), BF), pltpu.VMEM((tm, d), F32),
                        pltpu.VMEM((nj, hrows, cw), F32), pltpu.VMEM((nj, hrows, cw), F32)],
        compiler_params=_cparams("parallel", "arbitrary", "arbitrary"),
        name="conv_ffn",
    )(*args)
    def cols(z):
        return z.transpose(0, 2, 1, 3).reshape(n_outer, hrows, d_ff)

    return out, jnp.concatenate([cols(sa), cols(sb)], axis=-1)


def _compress_kernel(p_ref, pe_ref, w1_ref, w2_ref, o_ref, *, n):
    row = lax.broadcasted_iota(jnp.int32, (n, NSA_DH), 0)
    for g in range(NSA_GROUPS):
        x = p_ref[g]
        a = _dot((x + pe_ref[0:1, :]).astype(BF), w1_ref[0])
        b = _dot((x + pe_ref[1:2, :]).astype(BF), w1_ref[1])
        h = _silu(a + pltpu.roll(b, n - 1, 0))
        o = _dot(h.astype(BF), w2_ref[...])
        o_ref[g] = jnp.where(row < n - 1, o, 0.0)


def compress(pieces, pe_ab, w1_ab, w2):
    bsz, _, _, n, pw = pieces.shape
    return pl.pallas_call(
        functools.partial(_compress_kernel, n=n),
        grid=(bsz, 2),
        in_specs=[pl.BlockSpec((None, None, NSA_GROUPS, n, pw), lambda b, c: (b, c, 0, 0, 0)),
                  pl.BlockSpec((None, 2, pw), lambda b, c: (c, 0, 0)),
                  pl.BlockSpec((None, 2, pw, CMP_HIDDEN), lambda b, c: (c, 0, 0, 0)),
                  pl.BlockSpec((None, CMP_HIDDEN, NSA_DH), lambda b, c: (c, 0, 0))],
        out_specs=pl.BlockSpec((None, None, NSA_GROUPS, n, NSA_DH), lambda b, c: (b, c, 0, 0, 0)),
        out_shape=jax.ShapeDtypeStruct((bsz, 2, NSA_GROUPS, n, NSA_DH), F32),
        compiler_params=_cparams("parallel", "parallel"),
        name="nsa_compress",
    )(pieces, pe_ab, w1_ab, w2)


def _bias_softmax(s, bias):
    s = s + bias
    m = jnp.max(s, axis=-1, keepdims=True)
    p = jnp.exp(s - m)
    inv = jnp.where(m > 0.5 * NEG, 1.0 / jnp.sum(p, axis=-1, keepdims=True), 0.0)
    return p * inv


def _stack_heads(x):
    return jnp.concatenate([x] * NSA_REP, axis=0)


def _sum_heads(p, tq):
    out = p[0:tq]
    for r in range(1, NSA_REP):
        out = out + p[r * tq:(r + 1) * tq]
    return out


def _sel_scores(imp, blk, tp, ns):
    cur = tp // SEL_BLOCK
    forced = (blk == 0) | (blk == cur) | (blk == cur - 1)
    valid = blk * SEL_BLOCK <= tp
    score = jnp.where(forced, 1e6, jnp.where(valid, imp, -1e6))
    return jnp.where(blk < ns, score, -3e6)


def _top_n_mask(score, blk, axis, n):
    sel = jnp.zeros(score.shape, F32)
    for _ in range(n):
        mx = jnp.max(score, axis=axis, keepdims=True)
        first = jnp.min(jnp.where(score == mx, blk, LANES), axis=axis, keepdims=True)
        pick = blk == first
        sel = jnp.where(pick, 1.0, sel)
        score = jnp.where(pick, -3e38, score)
    return sel


def _gate_cols(gt, col0, br):
    return jnp.concatenate([gt[:, col0 + r * 3 + br:col0 + r * 3 + br + 1] for r in range(NSA_REP)], axis=0)


def _softmax_keys_on_rows(s):
    m = jnp.max(s, axis=0, keepdims=True)
    p = jnp.exp(s - m)
    inv = jnp.where(m > 0.5 * NEG, 1.0 / jnp.sum(p, axis=0, keepdims=True), 0.0)
    return p * inv


def _nsa_prompt_kernel(q_ref, gt_ref, kc_ref, vct_ref, ks_ref, vst_ref, kw_ref, vwt_ref, o_ref, bsel_scr,
                       *, tq, ncp, ns, ck, nw):
    rep, dh = NSA_REP, NSA_DH
    qb = pl.program_id(2) * tq
    n_chunks = (qb + tq - 1) // ck + 1
    ws = pl.multiple_of(jnp.maximum(qb - WINDOW, 0), tq)
    bpc = ck // LANES

    qt = (q_ref[...] * (dh ** -0.5)).T
    qt_cat = jnp.concatenate([qt[r * dh:(r + 1) * dh] for r in range(rep)], axis=1).astype(BF)

    def head(x, r):
        return x[:, r * tq:(r + 1) * tq]

    def tpos(shape):
        return lax.broadcasted_iota(jnp.int32, shape, 1) + qb

    n_idx = lax.broadcasted_iota(jnp.int32, (ncp, tq), 0)
    bias_c = jnp.where(CMP_STRIDE * n_idx + CMP_LEN - 1 <= tpos((ncp, tq)), 0.0, NEG)
    s_c = _dot(kc_ref[...], qt_cat)
    vct = vct_ref[...]
    o_c, psum = [], None
    for r in range(rep):
        p = _softmax_keys_on_rows(head(s_c, r) + bias_c)
        o_c.append(_dot(vct, p.astype(BF)))
        psum = p if psum is None else psum + p

    wpos = lax.broadcasted_iota(jnp.int32, (nw, tq), 0) + ws
    tw = tpos((nw, tq))
    bias_w = jnp.where((wpos <= tw) & (wpos > tw - WINDOW), 0.0, NEG)
    s_w = _dot(kw_ref[pl.ds(ws, nw), :], qt_cat)
    wblk = ws // LANES
    o_w = []
    for r in range(rep):
        p = _softmax_keys_on_rows(head(s_w, r) + bias_w).astype(BF)
        acc = _dot(vwt_ref[wblk], p[0:LANES])
        for jb in range(1, nw // LANES):
            acc = acc + _dot(vwt_ref[wblk + jb], p[jb * LANES:(jb + 1) * LANES])
        o_w.append(acc)

    cj = lax.broadcasted_iota(jnp.int32, (LANES, ncp), 0) * SEL_BLOCK
    cn = lax.broadcasted_iota(jnp.int32, (LANES, ncp), 1) * CMP_STRIDE
    cover_t = jnp.where((cn <= cj + SEL_BLOCK - 1) & (cn + CMP_LEN - 1 >= cj), 1.0, 0.0).astype(BF)
    hi, mid, lo = _split3(psum)
    imp_t = _dot(cover_t, hi) + _dot(cover_t, mid) + _dot(cover_t, lo)
    blk = lax.broadcasted_iota(jnp.int32, (LANES, tq), 0)
    sel_t = _top_n_mask(_sel_scores(imp_t, blk, tpos((LANES, tq)), ns), blk, 0, min(SEL_TOPN, ns))
    bsel_scr[...] = jnp.where(sel_t > 0.5, 0.0, NEG)

    bpk = ck // SEL_BLOCK

    def scores(c):
        return _dot(ks_ref[pl.ds(pl.multiple_of(c * ck, ck), ck), :], qt_cat)

    def sel_chunk(c, s, carry, causal):
        brows = bsel_scr[pl.ds(pl.multiple_of(c * bpk, bpk), bpk), :]
        if causal:
            kpos = lax.broadcasted_iota(jnp.int32, (ck, tq), 0) + c * ck
            cbias = jnp.where(kpos <= tpos((ck, tq)), 0.0, NEG)
        out = []
        for r in range(rep):
            m, l, acc = carry[r]
            sr = head(s, r)
            sr = jnp.concatenate(
                [sr[j * SEL_BLOCK:(j + 1) * SEL_BLOCK] + brows[j:j + 1] for j in range(bpk)], axis=0)
            if causal:
                sr = sr + cbias
            m_new = jnp.maximum(m, jnp.max(sr, axis=0, keepdims=True))
            alpha = jnp.exp(m - m_new)
            p = jnp.exp(sr - m_new)
            l = alpha * l + jnp.sum(p, axis=0, keepdims=True)
            p = p.astype(BF)
            acc = alpha * acc
            for jb in range(bpc):
                acc = acc + _dot(vst_ref[c * bpc + jb], p[jb * LANES:(jb + 1) * LANES])
            out.append((m_new, l, acc))
        return tuple(out)

    def pipelined(c, lc):
        s, carry = lc
        return scores(c + 1), sel_chunk(c, s, carry, False)

    init = tuple((jnp.full((1, tq), NEG, F32), jnp.zeros((1, tq), F32), jnp.zeros((dh, tq), F32))
                 for _ in range(rep))
    s_last, fin = lax.fori_loop(0, n_chunks - 1, pipelined, (scores(0), init))
    fin = sel_chunk(n_chunks - 1, s_last, fin, True)

    gt = _sigmoid(gt_ref[...])
    o_t = []
    for r in range(rep):
        _, l_s, acc_s = fin[r]
        o_s = acc_s / jnp.maximum(l_s, 1e-30)
        o_t.append(gt[3 * r:3 * r + 1] * o_c[r] + gt[3 * r + 1:3 * r + 2] * o_s + gt[3 * r + 2:3 * r + 3] * o_w[r])
    o_ref[...] = jnp.concatenate(o_t, axis=0).T


def nsa_attn_prompt(q, gates_t, kc, vct, ks, vst, kw, vwt, *, tq, ck):
    bsz, t_len, _ = q.shape
    ncp = kc.shape[2]
    gw = NSA_REP * NSA_DH
    ns = -(-t_len // SEL_BLOCK)
    assert tq == LANES and ck % LANES == 0 and (ck // SEL_BLOCK) % 8 == 0 and ck % tq == 0
    assert ns <= LANES and t_len % ck == 0 and t_len >= WINDOW + tq

    k_spec = pl.BlockSpec((None, None, t_len, NSA_DH), lambda b, g, i: (b, g, 0, 0))
    vt_spec = pl.BlockSpec((None, None, t_len // LANES, NSA_DH, LANES), lambda b, g, i: (b, g, 0, 0, 0))
    return pl.pallas_call(
        functools.partial(_nsa_prompt_kernel, tq=tq, ncp=ncp, ns=ns, ck=ck, nw=WINDOW + tq),
        scratch_shapes=[pltpu.VMEM((LANES, tq), F32)],
        grid=(bsz, NSA_GROUPS, t_len // tq),
        in_specs=[pl.BlockSpec((None, tq, gw), lambda b, g, i: (b, i, g)),
                  pl.BlockSpec((None, None, NSA_REP * 3, tq), lambda b, g, i: (b, g, 0, i)),
                  pl.BlockSpec((None, None, ncp, NSA_DH), lambda b, g, i: (b, g, 0, 0)),
                  pl.BlockSpec((None, None, NSA_DH, ncp), lambda b, g, i: (b, g, 0, 0)),
                  k_spec, vt_spec, k_spec, vt_spec],
        out_specs=pl.BlockSpec((None, tq, gw), lambda b, g, i: (b, i, g)),
        out_shape=jax.ShapeDtypeStruct((bsz, t_len, NSA_GROUPS * gw), F32),
        compiler_params=_cparams("parallel", "parallel", "arbitrary"),
        name="nsa_attn_prompt",
    )(q, gates_t, kc, vct, ks, vst, kw, vwt)


def _nsa_sample_kernel(pt_ref, q_ref, gt_ref, kvn_ref, wn_ref, *rest, n_pages, tq, past_len, wb):
    del pt_ref
    cmp_refs, sel_refs = rest[:n_pages], rest[n_pages:2 * n_pages]
    win_ref, pe_ref, w1_ref, w2_ref, o_ref = rest[2 * n_pages:]
    rep, dh, ng = NSA_REP, NSA_DH, NSA_GROUPS
    rows = rep * tq
    seg = PAGE_SIZE
    ppp = PAGE_SIZE // CMP_STRIDE
    ncp = n_pages * ppp
    nk = past_len + seg
    ns = -(-(past_len + tq) // SEL_BLOCK)

    def tpos(shape):
        return (lax.broadcasted_iota(jnp.int32, shape, 0) & (tq - 1)) + past_len

    kcvc = []
    for c in range(2):
        pieces = jnp.concatenate([cmp_refs[p][c, g] for g in range(ng) for p in range(n_pages)], axis=0)
        a = _dot((pieces + pe_ref[c, 0:1, :]).astype(BF), w1_ref[c, 0])
        b = _dot((pieces + pe_ref[c, 1:2, :]).astype(BF), w1_ref[c, 1])
        per_group = []
        for g in range(ng):
            h = _silu(a[g * ncp:(g + 1) * ncp] + pltpu.roll(b[g * ncp:(g + 1) * ncp], ncp - 1, 0))
            per_group.append(_dot(h.astype(BF), w2_ref[c]).astype(BF))
        kcvc.append(per_group)

    q = q_ref[...] * (dh ** -0.5)
    kvn = kvn_ref[...]
    wn = wn_ref[...]
    kvw = ng * dh

    def q4_of(g):
        return jnp.concatenate([q[:, (g * rep + r) * dh:(g * rep + r + 1) * dh] for r in range(rep)],
                               axis=0).astype(BF)

    def new_seg(x, col):
        return jnp.concatenate([x[:, col:col + dh], jnp.zeros((seg - tq, dh), F32)], axis=0).astype(BF)

    n_idx = lax.broadcasted_iota(jnp.int32, (rows, ncp), 1)
    bias_c = jnp.where(CMP_STRIDE * n_idx + CMP_LEN - 1 <= tpos((rows, ncp)), 0.0, NEG)
    q4s, o_cs, psums = [], [], []
    for g in range(ng):
        q4 = q4_of(g)
        p_c = _bias_softmax(_dot_nt(q4, kcvc[0][g]), bias_c)
        q4s.append(q4)
        o_cs.append(_dot(p_c.astype(BF), kcvc[1][g]))
        psums.append(_sum_heads(p_c, tq))
    cn = lax.broadcasted_iota(jnp.int32, (ncp, LANES), 0) * CMP_STRIDE
    cj = lax.broadcasted_iota(jnp.int32, (ncp, LANES), 1) * SEL_BLOCK
    cover = jnp.where((cn <= cj + SEL_BLOCK - 1) & (cn + CMP_LEN - 1 >= cj), 1.0, 0.0).astype(BF)
    hi, mid, lo = _split3(jnp.concatenate(psums, axis=0))
    imp = _dot(hi, cover) + _dot(mid, cover) + _dot(lo, cover)
    blk = lax.broadcasted_iota(jnp.int32, (ng * tq, LANES), 1)
    sel = _top_n_mask(_sel_scores(imp, blk, tpos((ng * tq, LANES)), ns), blk, 1, min(SEL_TOPN, ns))
    ej = lax.broadcasted_iota(jnp.int32, (LANES, nk), 0)
    ek = lax.broadcasted_iota(jnp.int32, (LANES, nk), 1)
    expand = jnp.where(ek // SEL_BLOCK == ej, 1.0, 0.0).astype(BF)
    sel_rows = jnp.concatenate([sel[g * tq:(g + 1) * tq] for g in range(ng) for _ in range(rep)], axis=0)
    mk_all = _dot(sel_rows.astype(BF), expand)
    kpos = lax.broadcasted_iota(jnp.int32, (rows, nk), 1)
    causal = kpos <= tpos((rows, nk))
    wp = lax.broadcasted_iota(jnp.int32, (rows, wb + seg), 1) + (past_len - wb)
    tw = tpos((rows, wb + seg))
    bias_w = jnp.where((wp <= tw) & (wp > tw - WINDOW), 0.0, NEG)

    gt = _sigmoid(gt_ref[...])
    for g in range(ng):
        q4 = q4s[g]
        sc = jnp.concatenate([_dot(q4, sel_refs[p][0, g]) for p in range(n_pages)]
                             + [_dot_nt(q4, new_seg(kvn, 2 * kvw + g * dh))], axis=1)
        bias = jnp.where(causal & (mk_all[g * rows:(g + 1) * rows] > 0.5), 0.0, NEG)
        p_s = _bias_softmax(sc, bias).astype(BF)
        o_s = _dot(p_s[:, past_len:], new_seg(kvn, 3 * kvw + g * dh))
        for p in range(n_pages):
            o_s = o_s + _dot_nt(p_s[:, p * seg:(p + 1) * seg], sel_refs[p][1, g])
        sw = jnp.concatenate([_dot(q4, win_ref[0, g]), _dot_nt(q4, new_seg(wn, g * dh))], axis=1)
        p_w = _bias_softmax(sw, bias_w).astype(BF)
        o_w = _dot_nt(p_w[:, :wb], win_ref[1, g]) + _dot(p_w[:, wb:], new_seg(wn, kvw + g * dh))
        c0 = g * rep * 3
        o4 = _gate_cols(gt, c0, 0) * o_cs[g] + _gate_cols(gt, c0, 1) * o_s + _gate_cols(gt, c0, 2) * o_w
        o_ref[:, g * rep * dh:(g + 1) * rep * dh] = jnp.concatenate(
            [o4[r * tq:(r + 1) * tq] for r in range(rep)], axis=1)


def nsa_attn_sample(q, gates, kv_new, win_new, cmp_pages, sel_pages, win_t, page_table, pe_ab, w1_ab, w2):
    bsz, tq, hq = q.shape
    n_pages = page_table.shape[1]
    wb = win_t.shape[-1]
    assert tq & (tq - 1) == 0 and tq <= PAGE_SIZE

    def row_spec(w):
        return pl.BlockSpec((None, tq, w), lambda b, pt: (b, 0, 0))

    def page_spec(shape, p):
        return pl.BlockSpec((None,) + shape, lambda b, pt: (pt[b, p],) + (0,) * len(shape))

    def full_spec(x):
        return pl.BlockSpec(x.shape, lambda b, pt: (0,) * x.ndim)

    in_specs = ([row_spec(hq), row_spec(gates.shape[2]), row_spec(kv_new.shape[2]), row_spec(win_new.shape[2])]
                + [page_spec(cmp_pages.shape[1:], p) for p in range(n_pages)]
                + [page_spec(sel_pages.shape[1:], p) for p in range(n_pages)]
                + [pl.BlockSpec((None,) + win_t.shape[1:], lambda b, pt: (b, 0, 0, 0, 0)),
                   full_spec(pe_ab), full_spec(w1_ab), full_spec(w2)])
    return pl.pallas_call(
        functools.partial(_nsa_sample_kernel, n_pages=n_pages, tq=tq, past_len=n_pages * PAGE_SIZE, wb=wb),
        grid_spec=pltpu.PrefetchScalarGridSpec(
            num_scalar_prefetch=1, grid=(bsz,), in_specs=in_specs, out_specs=row_spec(hq)),
        out_shape=jax.ShapeDtypeStruct((bsz, tq, hq), F32),
        compiler_params=_cparams("parallel"),
        name="nsa_attn_sample",
    )(page_table, q, gates, kv_new, win_new, *([cmp_pages] * n_pages), *([sel_pages] * n_pages),
      win_t, pe_ab, w1_ab, w2)


def _pad_cols(w, n):
    return jnp.pad(w, ((0, 0), (0, n - w.shape[1])))


def _group_major(x, bsz, t_len):
    return x.reshape(bsz, t_len, NSA_GROUPS, NSA_DH).transpose(0, 2, 1, 3)


def _group_major_t(x, bsz, t_len):
    x5 = x.reshape(bsz, t_len // LANES, LANES, NSA_GROUPS, NSA_DH)
    return x5.transpose(0, 3, 1, 4, 2)


def _ffn_chunk(d_ff):
    n = d_ff // LANES
    for parts in range(2, n + 1):
        if n % parts == 0:
            return (n // parts) * LANES
    return d_ff


def _to_time_major(x, bsz, t_len):
    return x.reshape(bsz, t_len, -1).transpose(1, 0, 2).reshape(t_len * bsz, -1)


def _to_batch_major(x, bsz, t_len):
    return x.reshape(t_len, bsz, -1).transpose(1, 0, 2).reshape(bsz * t_len, -1)


def kernel(x_prompt, x_sample, cache_gla_state, cache_conv, cache_nsa_kv, cache_nsa_win, cache_mem_kv,
           cache_ffn_conv, page_table, mem_prompt, norm_mix, norm_mem, norm_x, norm_ffn, norm_final,
           w_in_a, w_gate_a, b_gate_a, g_gla_out, w_dw_b, b_dw_b, g_ln_b, b_ln_b, w_out_a,
           w_in_c, pe_cmp, w_cmp1, w_cmp2, w_out_c, w_xq, w_mem_kv, w_xo, w_up, w_ffn_dw, b_ffn_dw, w_down):
    bp, tp, d = x_prompt.shape
    bs, ts, _ = x_sample.shape
    depth = norm_mix.shape[0]
    n_mem = mem_prompt.shape[1]
    d_ff = w_down.shape[1]
    hk, hv = GLA_HEADS * GLA_DK, GLA_HEADS * GLA_DV
    kvw = NSA_GROUPS * NSA_DH
    hq = NSA_HEADS * NSA_DH

    xp = x_prompt.reshape(bp * tp, d)
    xs = x_sample.reshape(bs * ts, d)
    tm_p = 512
    tm_s = bs * ts

    outs = {k: [] for k in ("gla_p", "gla_s", "conv_p", "conv_s", "nsa_p", "nsa_s", "win_p", "win_s",
                            "mem_p", "ffn_p", "ffn_s")}
    for l in range(depth):
        i = l // 2
        if l % 2 == 0:
            wi = w_in_a[i]
            c0, c1, c2, c3, c4 = hk, 2 * hk, 2 * hk + hv, 2 * hk + 2 * hv, 2 * hk + 2 * hv + GLA_RANK
            w_a = jnp.concatenate([wi[:, c4:], wi[:, c1:c2], wi[:, c2:c3], wi[:, :c0], wi[:, c0:c1],
                                   _pad_cols(wi[:, c3:c4], LANES)], axis=1).astype(BF)
            splits = (2 * CONV_CH, hv, hv, hk, hk, LANES)
            wg = jnp.pad(w_gate_a[i], ((0, LANES - GLA_RANK), (0, 0))).astype(BF)
            bg = b_gate_a[i].reshape(1, hk)
            go = g_gla_out[i].reshape(1, hv)
            w_out = w_out_a[i].astype(BF)
            vecs = [v.reshape(1, CONV_CH) for v in (b_dw_b[i], g_ln_b[i], b_ln_b[i])]

            def mixer(x, bsz, t_len, tm, s0, hist_tm, n_outer, ctm, step):
                u, v, r, q, k, a = norm_matmul(x, norm_mix[l], w_a, splits, tm)
                sh = lambda z: z.reshape(bsz, t_len, -1)
                og, s_new = gla(sh(q), sh(k), sh(v), sh(r), sh(a), wg, bg, go, s0, 1024)
                if step > 1:
                    u = _to_time_major(u, bsz, t_len)
                c, hist_new = convmod(u, hist_tm, w_dw_b[i], *vecs, n_outer, ctm, step)
                if step > 1:
                    c = _to_batch_major(c, bsz, t_len)
                y = matmul_res([og.reshape(bsz * t_len, hv), c], [w_out[:hv], w_out[hv:]], x, tm)
                return y, s_new, hist_new

            xp, sp, cp = mixer(xp, bp, tp, tm_p, jnp.zeros((bp, GLA_HEADS, GLA_DK, GLA_DV), F32),
                               jnp.zeros((bp, CONV_WIDTH - 1, CONV_CH), F32), bp, 512, 1)
            hist_s = cache_conv[i].transpose(1, 0, 2).reshape(1, (CONV_WIDTH - 1) * bs, CONV_CH)
            xs, ss, cs = mixer(xs, bs, ts, tm_s, cache_gla_state[i], hist_s, 1, bs * ts, bs)
            cs = cs.reshape(CONV_WIDTH - 1, bs, CONV_CH).transpose(1, 0, 2)
            outs["gla_p"].append(sp)
            outs["gla_s"].append(ss)
            outs["conv_p"].append(cp)
            outs["conv_s"].append(cs)
        else:
            n_in = hq + 6 * kvw + 3 * NSA_HEADS
            n_pad = -(-n_in // LANES) * LANES
            w_c = _pad_cols(w_in_c[i], n_pad).astype(BF)
            splits = (hq, 4 * kvw, 2 * kvw, n_pad - hq - 6 * kvw)
            w_out = w_out_c[i].astype(BF)
            pw = CMP_STRIDE * NSA_DH
            pe_ab = pe_cmp[i].reshape(2, 2, pw)
            w1_ab = w_cmp1[i].reshape(2, 2, pw, CMP_HIDDEN).astype(BF)
            w2 = w_cmp2[i].astype(BF)

            def pieces_of(rows, bsz, n):
                r6 = rows[:, :n * CMP_STRIDE].reshape(bsz, n, CMP_STRIDE, 2, NSA_GROUPS, NSA_DH)
                return r6.transpose(0, 3, 4, 1, 2, 5).reshape(bsz, 2, NSA_GROUPS, n, pw)

            q, kv4, win, gl = norm_matmul(xp, norm_mix[l], w_c, splits, tm_p)
            kv4b = kv4.reshape(bp, tp, 4 * kvw)
            kcvc = compress(pieces_of(kv4b[:, :, :2 * kvw], bp, tp // CMP_STRIDE), pe_ab, w1_ab, w2)
            ks = _group_major(kv4[:, 2 * kvw:3 * kvw], bp, tp).astype(BF)
            vst = _group_major_t(kv4[:, 3 * kvw:], bp, tp).astype(BF)
            kw = _group_major(win[:, :kvw], bp, tp).astype(BF)
            vwt = _group_major_t(win[:, kvw:], bp, tp).astype(BF)
            gates_t = gl[:, :3 * NSA_HEADS].reshape(bp, tp, NSA_GROUPS, NSA_REP * 3).transpose(0, 2, 3, 1)
            o = nsa_attn_prompt(q.reshape(bp, tp, hq), gates_t, kcvc[:, 0].astype(BF),
                                kcvc[:, 1].transpose(0, 1, 3, 2).astype(BF), ks, vst, kw, vwt,
                                tq=Q_BLOCK, ck=512)
            xp = matmul_res([o.reshape(bp * tp, hq)], [w_out], xp, tm_p)
            win_rows = min(WINDOW, tp)
            outs["nsa_p"].append(kv4.reshape(bp, tp, 4, NSA_GROUPS, NSA_DH))
            outs["win_p"].append(win.reshape(bp, tp, 2, NSA_GROUPS, NSA_DH)[:, tp - win_rows:])

            q, kv4, win, gl = norm_matmul(xs, norm_mix[l], w_c, splits, tm_s)
            pool = cache_nsa_kv[i]
            n_phys = pool.shape[0]
            ppp = PAGE_SIZE // CMP_STRIDE
            cmp_pages = (pool[:, :, :2].reshape(n_phys, ppp, CMP_STRIDE, 2, NSA_GROUPS, NSA_DH)
                         .transpose(0, 3, 4, 1, 2, 5).reshape(n_phys, 2, NSA_GROUPS, ppp, pw))
            sel_pages = pool[:, :, 2:].transpose(0, 2, 3, 4, 1).astype(BF)
            win_cache = cache_nsa_win[i]
            wb = win_cache.shape[1]
            win_t = win_cache.transpose(0, 2, 3, 4, 1).astype(BF)
            o = nsa_attn_sample(q.reshape(bs, ts, hq), gl.reshape(bs, ts, -1), kv4.reshape(bs, ts, 4 * kvw),
                                win.reshape(bs, ts, 2 * kvw), cmp_pages, sel_pages, win_t, page_table,
                                pe_ab, w1_ab, w2)
            xs = matmul_res([o.reshape(bs * ts, hq)], [w_out], xs, tm_s)
            outs["nsa_s"].append(kv4.reshape(bs, ts, 4, NSA_GROUPS, NSA_DH))
            win_new = win.reshape(bs, ts, 2, NSA_GROUPS, NSA_DH)
            outs["win_s"].append(jnp.concatenate([win_cache, win_new], axis=1)[:, ts:])

        xhw = X_HEADS * X_DH
        (mkv,) = norm_matmul(mem_prompt.reshape(bp * n_mem, d), norm_mem[l], w_mem_kv[l].astype(BF),
                             (2 * xhw,), bp * n_mem)
        outs["mem_p"].append(mkv.reshape(bp, n_mem, 2, X_HEADS, X_DH))
        w_q = w_xq[l].astype(BF)
        w_o = w_xo[l].astype(BF)
        (qx,) = norm_matmul(xp, norm_x[l], w_q, (xhw,), tm_p)
        ox = xattn(qx.reshape(bp, tp, xhw), mkv.reshape(1, bp, n_mem, 2, X_HEADS, X_DH), 0, 512)
        xp = matmul_res([ox.reshape(bp * tp, xhw)], [w_o], xp, tm_p)
        (qx,) = norm_matmul(xs, norm_x[l], w_q, (xhw,), tm_s)
        ox = xattn(qx.reshape(bs, ts, xhw), cache_mem_kv, l, ts)
        xs = matmul_res([ox.reshape(bs * ts, xhw)], [w_o], xs, tm_s)

        wu = w_up[l].astype(BF)
        wd = w_down[l].astype(BF)
        fg = norm_final if l == depth - 1 else None
        cw = _ffn_chunk(d_ff)
        xp, hfp = conv_ffn(xp, norm_ffn[l], jnp.zeros((bp, FFN_WIDTH - 1, 2 * d_ff), F32), wu, w_ffn_dw[l],
                           b_ffn_dw[l], wd, bp, 512, 1, cw, fg)
        hist_s = cache_ffn_conv[l].transpose(1, 0, 2).reshape(1, (FFN_WIDTH - 1) * bs, 2 * d_ff)
        xs_tm, hfs = conv_ffn(_to_time_major(xs, bs, ts), norm_ffn[l], hist_s, wu, w_ffn_dw[l], b_ffn_dw[l], wd,
                              1, bs * ts, bs, cw, fg)
        xs = _to_batch_major(xs_tm, bs, ts)
        outs["ffn_p"].append(hfp)
        outs["ffn_s"].append(hfs.reshape(FFN_WIDTH - 1, bs, 2 * d_ff).transpose(1, 0, 2))

    st = jnp.stack
    return (xp.reshape(bp, tp, d), xs.reshape(bs, ts, d),
            st(outs["gla_p"]), st(outs["gla_s"]), st(outs["conv_p"]), st(outs["conv_s"]),
            st(outs["nsa_p"]), st(outs["nsa_s"]), st(outs["win_p"]), st(outs["win_s"]),
            st(outs["mem_p"]), st(outs["ffn_p"]), st(outs["ffn_s"]))
```

```python
import functools
import math

import jax
import jax.numpy as jnp
from jax import lax
from jax.experimental import pallas as pl
from jax.experimental.pallas import tpu as pltpu

EPS = 1e-6
NEG = -1e30
F32 = jnp.float32
BF = jnp.bfloat16

V7X_VMEM_BYTES = 64 * 1024 * 1024
VMEM_LIMIT = V7X_VMEM_BYTES - 8 * 1024 * 1024
LANES = 128

GLA_HEADS, GLA_DK, GLA_DV, GLA_RANK, GLA_TAU, GLA_CHUNK = 4, 64, 128, 16, 16.0, 16
GLA_UNROLL = 4
CONV_CH, CONV_WIDTH = 512, 31
NSA_HEADS, NSA_GROUPS, NSA_DH = 16, 4, 64
NSA_REP = NSA_HEADS // NSA_GROUPS
CMP_LEN, CMP_STRIDE, CMP_HIDDEN = 32, 16, 64
SEL_BLOCK, SEL_TOPN, WINDOW, Q_BLOCK = 64, 16, 512, 128
X_HEADS, X_DH = 4, 128
FFN_WIDTH = 3
PAGE_SIZE = 128


def _cparams(*sem):
    return pltpu.CompilerParams(dimension_semantics=sem, vmem_limit_bytes=VMEM_LIMIT)


def _rms(x, g):
    return x * lax.rsqrt(jnp.mean(x * x, axis=-1, keepdims=True) + EPS) * g


def _sigmoid(x):
    return 1.0 / (1.0 + jnp.exp(-x))


def _silu(x):
    return x * _sigmoid(x)


def _dot(a, b):
    return jnp.dot(a, b, preferred_element_type=F32)


def _dot_nt(a, b):
    return lax.dot_general(a, b, (((1,), (1,)), ((), ())), preferred_element_type=F32)


def _dot_tn(a, b):
    return lax.dot_general(a, b, (((0,), (0,)), ((), ())), preferred_element_type=F32)


def _split3(x):
    hi = x.astype(BF)
    r1 = x - hi.astype(F32)
    mid = r1.astype(BF)
    lo = (r1 - mid.astype(F32)).astype(BF)
    return hi, mid, lo


def _norm_matmul_kernel(x_ref, g_ref, w_ref, *o_refs, splits):
    xn = _rms(x_ref[...], g_ref[...]).astype(BF)
    off = 0
    for o_ref, n in zip(o_refs, splits):
        o_ref[...] = _dot(xn, w_ref[:, off:off + n])
        off += n


def norm_matmul(x, g, w, splits, tm):
    m, d = x.shape
    n = w.shape[1]
    assert sum(splits) == n and m % tm == 0
    outs = pl.pallas_call(
        functools.partial(_norm_matmul_kernel, splits=tuple(splits)),
        grid=(m // tm,),
        in_specs=[pl.BlockSpec((tm, d), lambda i: (i, 0)),
                  pl.BlockSpec((1, d), lambda i: (0, 0)),
                  pl.BlockSpec((d, n), lambda i: (0, 0))],
        out_specs=[pl.BlockSpec((tm, s), lambda i: (i, 0)) for s in splits],
        out_shape=[jax.ShapeDtypeStruct((m, s), F32) for s in splits],
        compiler_params=_cparams("parallel"),
        name="norm_matmul",
    )(x, g.reshape(1, d), w)
    return outs


def _matmul_res_kernel(*refs, n_in):
    a_refs, w_refs = refs[:n_in], refs[n_in:2 * n_in]
    res_ref, o_ref = refs[2 * n_in], refs[2 * n_in + 1]
    acc = _dot(a_refs[0][...].astype(BF), w_refs[0][...])
    for a_ref, w_ref in zip(a_refs[1:], w_refs[1:]):
        acc = acc + _dot(a_ref[...].astype(BF), w_ref[...])
    o_ref[...] = res_ref[...] + acc


def matmul_res(a_list, w_list, res, tm):
    m, d = res.shape
    n_in = len(a_list)
    return pl.pallas_call(
        functools.partial(_matmul_res_kernel, n_in=n_in),
        grid=(m // tm,),
        in_specs=([pl.BlockSpec((tm, a.shape[1]), lambda i: (i, 0)) for a in a_list]
                  + [pl.BlockSpec(w.shape, lambda i: (0, 0)) for w in w_list]
                  + [pl.BlockSpec((tm, d), lambda i: (i, 0))]),
        out_specs=pl.BlockSpec((tm, d), lambda i: (i, 0)),
        out_shape=jax.ShapeDtypeStruct((m, d), F32),
        compiler_params=_cparams("parallel"),
        name="matmul_res",
    )(*a_list, *w_list, res)


def _gla_kernel(q_ref, k_ref, v_ref, r_ref, a_ref, wg_ref, bg_ref, go_ref, s0_ref,
                o_ref, sout_ref, s_scr, b_scr, *, chunk, n_chunks):
    j = pl.program_id(1)
    tt = chunk * n_chunks
    blk = min(LANES, tt)

    @pl.when(j == 0)
    def _():
        s_scr[...] = s0_ref[...]

    ri = lax.broadcasted_iota(jnp.int32, (chunk, chunk), 0)
    ci = lax.broadcasted_iota(jnp.int32, (chunk, chunk), 1)
    tri = (ri >= ci).astype(F32)
    eye = (lax.broadcasted_iota(jnp.int32, (GLA_DK, GLA_DK), 0)
           == lax.broadcasted_iota(jnp.int32, (GLA_DK, GLA_DK), 1)).astype(F32)

    rb = lax.broadcasted_iota(jnp.int32, (blk, blk), 0)
    cb = lax.broadcasted_iota(jnp.int32, (blk, blk), 1)
    tri_blk = jnp.where((rb >= cb) & (rb // chunk == cb // chunk), 1.0, 0.0).astype(BF)
    for nb in range(tt // blk):
        rs = slice(nb * blk, (nb + 1) * blk)
        z = _dot(a_ref[rs, :].astype(BF), wg_ref[...]) + bg_ref[...]
        log_a = (jnp.minimum(z, 0.0) - jnp.log(1.0 + jnp.exp(-jnp.abs(z)))) / GLA_TAU
        hi, mid, lo = _split3(log_a)
        b_scr[rs, :] = _dot(tri_blk, hi) + _dot(tri_blk, mid) + _dot(tri_blk, lo)

    def body(i, carry):
        sl = pl.ds(pl.multiple_of(i * chunk, chunk), chunk)
        qc = q_ref[sl, :] * (GLA_DK ** -0.5)
        kc = k_ref[sl, :]
        b = b_scr[sl, :]
        new_s = []
        for h in range(GLA_HEADS):
            ks = slice(h * GLA_DK, (h + 1) * GLA_DK)
            vs = slice(h * GLA_DV, (h + 1) * GLA_DV)
            bh = b[:, ks]
            b_last = bh[chunk - 1:chunk, :]
            qe = (qc[:, ks] * jnp.exp(bh)).astype(BF)
            ke = (kc[:, ks] * jnp.exp(-bh)).astype(BF)
            kl = (kc[:, ks] * jnp.exp(b_last - bh)).astype(BF)
            vh = v_ref[sl, vs].astype(BF)
            s = carry[h]
            att = _dot_nt(qe, ke) * tri
            o = _dot(att.astype(BF), vh) + _dot(qe, s.astype(BF))
            decay = jnp.exp(jnp.sum(eye * b_last, axis=1, keepdims=True))
            new_s.append(decay * s + _dot_tn(kl, vh))
            o = o * lax.rsqrt(jnp.mean(o * o, axis=-1, keepdims=True) + EPS) * go_ref[:, vs]
            o_ref[sl, vs] = o * _silu(r_ref[sl, vs])
        return tuple(new_s)

    s_fin = lax.fori_loop(0, n_chunks, body, tuple(s_scr[h] for h in range(GLA_HEADS)),
                          unroll=math.gcd(n_chunks, GLA_UNROLL))
    for h in range(GLA_HEADS):
        s_scr[h] = s_fin[h]
    sout_ref[...] = s_scr[...]


def gla(q, k, v, r, a, wg, bg, go, s0, tt):
    bsz, t_len, _ = q.shape
    chunk = math.gcd(t_len, GLA_CHUNK)
    tt = min(tt, t_len)
    hk, hv = GLA_HEADS * GLA_DK, GLA_HEADS * GLA_DV

    def tspec(w):
        return pl.BlockSpec((None, tt, w), lambda b, j: (b, j, 0))

    sspec = pl.BlockSpec((None, GLA_HEADS, GLA_DK, GLA_DV), lambda b, j: (b, 0, 0, 0))
    return pl.pallas_call(
        functools.partial(_gla_kernel, chunk=chunk, n_chunks=tt // chunk),
        grid=(bsz, t_len // tt),
        in_specs=[tspec(hk), tspec(hk), tspec(hv), tspec(hv), tspec(LANES),
                  pl.BlockSpec((LANES, hk), lambda b, j: (0, 0)),
                  pl.BlockSpec((1, hk), lambda b, j: (0, 0)),
                  pl.BlockSpec((1, hv), lambda b, j: (0, 0)),
                  sspec],
        out_specs=[tspec(hv), sspec],
        out_shape=[jax.ShapeDtypeStruct((bsz, t_len, hv), F32),
                   jax.ShapeDtypeStruct((bsz, GLA_HEADS, GLA_DK, GLA_DV), F32)],
        scratch_shapes=[pltpu.VMEM((GLA_HEADS, GLA_DK, GLA_DV), F32), pltpu.VMEM((tt, hk), F32)],
        compiler_params=_cparams("parallel", "arbitrary"),
        name="gla",
    )(q, k, v, r, a, wg, bg, go, s0)


def _convmod_kernel(u1_ref, u2_ref, hist_ref, w_ref, b_ref, g_ref, bl_ref, c_ref, hout_ref, gpad,
                    *, tm, step, off):
    hrows = (CONV_WIDTH - 1) * step
    base = off - hrows

    @pl.when(pl.program_id(1) == 0)
    def _():
        gpad[base:off, :] = hist_ref[...]

    gpad[off:off + tm, :] = u1_ref[...] * _sigmoid(u2_ref[...])
    acc = b_ref[...] + w_ref[0:1, :] * gpad[base:base + tm, :]
    for kk in range(1, CONV_WIDTH):
        acc = acc + w_ref[kk:kk + 1, :] * gpad[base + kk * step:base + kk * step + tm, :]
    d = acc - jnp.mean(acc, axis=-1, keepdims=True)
    c = d * lax.rsqrt(jnp.mean(d * d, axis=-1, keepdims=True) + EPS) * g_ref[...] + bl_ref[...]
    c_ref[...] = _silu(c)
    new_hist = gpad[base + tm:off + tm, :]
    hout_ref[...] = new_hist
    gpad[base:off, :] = new_hist


def convmod(u, hist, w, b, g, bl, n_outer, tm, step):
    m = u.shape[0]
    rows = m // n_outer
    nt = rows // tm
    hrows = (CONV_WIDTH - 1) * step
    assert tm >= hrows or nt == 1
    off = -(-hrows // 8) * 8
    ch = CONV_CH
    vec = pl.BlockSpec((1, ch), lambda o, t: (0, 0))
    return pl.pallas_call(
        functools.partial(_convmod_kernel, tm=tm, step=step, off=off),
        grid=(n_outer, nt),
        in_specs=[pl.BlockSpec((tm, ch), lambda o, t: (o * nt + t, 0)),
                  pl.BlockSpec((tm, ch), lambda o, t: (o * nt + t, 1)),
                  pl.BlockSpec((None, hrows, ch), lambda o, t: (o, 0, 0)),
                  pl.BlockSpec((CONV_WIDTH, ch), lambda o, t: (0, 0)),
                  vec, vec, vec],
        out_specs=[pl.BlockSpec((tm, ch), lambda o, t: (o * nt + t, 0)),
                   pl.BlockSpec((None, hrows, ch), lambda o, t: (o, 0, 0))],
        out_shape=[jax.ShapeDtypeStruct((m, ch), F32),
                   jax.ShapeDtypeStruct((n_outer, hrows, ch), F32)],
        scratch_shapes=[pltpu.VMEM((off + tm, ch), F32)],
        compiler_params=_cparams("parallel", "arbitrary"),
        name="convmod",
    )(u, u, hist, w, b, g, bl)


def _xattn_kernel(q_ref, kv_ref, o_ref):
    q = q_ref[...] * (X_DH ** -0.5)
    for h in range(X_HEADS):
        sl = slice(h * X_DH, (h + 1) * X_DH)
        kh = kv_ref[:, 0, h, :].astype(BF)
        vh = kv_ref[:, 1, h, :].astype(BF)
        s = _dot_nt(q[:, sl].astype(BF), kh)
        p = jnp.exp(s - jnp.max(s, axis=-1, keepdims=True))
        p = p / jnp.sum(p, axis=-1, keepdims=True)
        o_ref[:, sl] = _dot(p.astype(BF), vh)


def xattn(q, kv, layer, tq):
    bsz, t_len, hw = q.shape
    n_mem = kv.shape[2]
    return pl.pallas_call(
        _xattn_kernel,
        grid=(bsz, t_len // tq),
        in_specs=[pl.BlockSpec((None, tq, hw), lambda b, i: (b, i, 0)),
                  pl.BlockSpec((None, None, n_mem, 2, X_HEADS, X_DH), lambda b, i: (layer, b, 0, 0, 0, 0))],
        out_specs=pl.BlockSpec((None, tq, hw), lambda b, i: (b, i, 0)),
        out_shape=jax.ShapeDtypeStruct((bsz, t_len, hw), F32),
        compiler_params=_cparams("parallel", "parallel"),
        name="xattn",
    )(q, kv)


def _ffn_kernel(*refs, tm, step, final_norm):
    (x_ref, g_ref, ha_ref, hb_ref, wua_ref, wub_ref, wda_ref, wdb_ref, ba_ref, bb_ref, wdn_ref) = refs[:11]
    rest = refs[11:]
    if final_norm:
        gf_ref, rest = rest[0], rest[1:]
    o_ref, sa_ref, sb_ref, xn_scr, acc_scr, halo_a, halo_b = rest
    t = pl.program_id(1)
    j = pl.program_id(2)
    hrows = (FFN_WIDTH - 1) * step

    @pl.when(j == 0)
    def _():
        xn_scr[...] = _rms(x_ref[...], g_ref[...]).astype(BF)

    @pl.when(t == 0)
    def _():
        halo_a[j] = ha_ref[...]
        halo_b[j] = hb_ref[...]

    def conv_half(wu_ref, wd_ref, b_ref, halo, s_ref):
        u = _dot(xn_scr[...], wu_ref[...])
        h = halo[j]
        if step % 8 == 0:
            u1 = jnp.concatenate([h[step:], u[:tm - step]], axis=0)
            u2 = jnp.concatenate([h, u[:tm - 2 * step]], axis=0)
        else:
            assert step == 1
            row = lax.broadcasted_iota(jnp.int32, (8, u.shape[1]), 0)
            r1, r2 = pltpu.roll(u, 1, 0), pltpu.roll(u, 2, 0)
            top1 = jnp.where(row == 0, h[1:2, :], r1[0:8])
            top2 = jnp.where(row == 0, h[0:1, :], jnp.where(row == 1, h[1:2, :], r2[0:8]))
            u1 = jnp.concatenate([top1, r1[8:]], axis=0)
            u2 = jnp.concatenate([top2, r2[8:]], axis=0)
        c = b_ref[...] + wd_ref[0:1, :] * u2 + wd_ref[1:2, :] * u1 + wd_ref[2:3, :] * u
        new_h = u[tm - hrows:, :]
        halo[j] = new_h
        s_ref[j] = new_h
        return c

    ca = conv_half(wua_ref, wda_ref, ba_ref, halo_a, sa_ref)
    cb = conv_half(wub_ref, wdb_ref, bb_ref, halo_b, sb_ref)
    contrib = _dot((_silu(ca) * cb).astype(BF), wdn_ref[...])

    @pl.when(j == 0)
    def _():
        acc_scr[...] = x_ref[...] + contrib

    @pl.when(j > 0)
    def _():
        acc_scr[...] = acc_scr[...] + contrib

    @pl.when(j == pl.num_programs(2) - 1)
    def _():
        if final_norm:
            o_ref[...] = _rms(acc_scr[...], gf_ref[...])
        else:
            o_ref[...] = acc_scr[...]


def conv_ffn(x, g, hist, w_up, w_dw, b_dw, w_down, n_outer, tm, step, cw, final_g=None):
    m, d = x.shape
    d_ff = w_down.shape[0]
    rows = m // n_outer
    nt = rows // tm
    nj = d_ff // cw
    hrows = (FFN_WIDTH - 1) * step
    assert tm >= hrows and d_ff % cw == 0
    final_norm = final_g is not None

    in_specs = [
        pl.BlockSpec((tm, d), lambda o, t, j: (o * nt + t, 0)),
        pl.BlockSpec((1, d), lambda o, t, j: (0, 0)),
        pl.BlockSpec((None, hrows, cw), lambda o, t, j: (o, 0, j)),
        pl.BlockSpec((None, hrows, cw), lambda o, t, j: (o, 0, nj + j)),
        pl.BlockSpec((d, cw), lambda o, t, j: (0, j)),
        pl.BlockSpec((d, cw), lambda o, t, j: (0, nj + j)),
        pl.BlockSpec((FFN_WIDTH, cw), lambda o, t, j: (0, j)),
        pl.BlockSpec((FFN_WIDTH, cw), lambda o, t, j: (0, nj + j)),
        pl.BlockSpec((1, cw), lambda o, t, j: (0, j)),
        pl.BlockSpec((1, cw), lambda o, t, j: (0, nj + j)),
        pl.BlockSpec((cw, d), lambda o, t, j: (j, 0)),
    ]
    args = [x, g.reshape(1, d), hist, hist, w_up, w_up, w_dw, w_dw, b_dw.reshape(1, -1), b_dw.reshape(1, -1), w_down]
    if final_norm:
        in_specs.append(pl.BlockSpec((1, d), lambda o, t, j: (0, 0)))
        args.append(final_g.reshape(1, d))
    out, sa, sb = pl.pallas_call(
        functools.partial(_ffn_kernel, tm=tm, step=step, final_norm=final_norm),
        grid=(n_outer, nt, nj),
        in_specs=in_specs,
        out_specs=[pl.BlockSpec((tm, d), lambda o, t, j: (o * nt + t, 0)),
                   pl.BlockSpec((None, nj, hrows, cw), lambda o, t, j: (o, 0, 0, 0)),
                   pl.BlockSpec((None, nj, hrows, cw), lambda o, t, j: (o, 0, 0, 0))],
        out_shape=[jax.ShapeDtypeStruct((m, d), F32),
                   jax.ShapeDtypeStruct((n_outer, nj, hrows, cw), F32),
                   jax.ShapeDtypeStruct((n_outer, nj, hrows, cw), F32)],
        scratch_shapes=[pltpu.VMEM((tm, d), BF), pltpu.VMEM((tm, d), F32),
                        pltpu.VMEM((nj, hrows, cw), F32), pltpu.VMEM((nj, hrows, cw), F32)],
        compiler_params=_cparams("parallel", "arbitrary", "arbitrary"),
        name="conv_ffn",
    )(*args)
    def cols(z):
        return z.transpose(0, 2, 1, 3).reshape(n_outer, hrows, d_ff)

    return out, jnp.concatenate([cols(sa), cols(sb)], axis=-1)


def _compress_halves(x, pe2, w1cat):
    pw = _dot(pe2.astype(BF), w1cat)
    const = jnp.concatenate([pw[0:1, :CMP_HIDDEN], pw[1:2, CMP_HIDDEN:]], axis=1)
    ab = _dot(x.astype(BF), w1cat) + const
    return ab[:, :CMP_HIDDEN], ab[:, CMP_HIDDEN:]


def _compress_kernel(p_ref, pe_ref, w1_ref, w2_ref, o_ref, *, n):
    row = lax.broadcasted_iota(jnp.int32, (n, NSA_DH), 0)
    for g in range(NSA_GROUPS):
        a, b = _compress_halves(p_ref[g], pe_ref[...], w1_ref[...])
        h = _silu(a + pltpu.roll(b, n - 1, 0))
        o = _dot(h.astype(BF), w2_ref[...])
        o_ref[g] = jnp.where(row < n - 1, o, 0.0)


def compress(pieces, pe_ab, w1_cat, w2):
    bsz, _, _, n, pw = pieces.shape
    return pl.pallas_call(
        functools.partial(_compress_kernel, n=n),
        grid=(bsz, 2),
        in_specs=[pl.BlockSpec((None, None, NSA_GROUPS, n, pw), lambda b, c: (b, c, 0, 0, 0)),
                  pl.BlockSpec((None, 2, pw), lambda b, c: (c, 0, 0)),
                  pl.BlockSpec((None, pw, 2 * CMP_HIDDEN), lambda b, c: (c, 0, 0)),
                  pl.BlockSpec((None, CMP_HIDDEN, NSA_DH), lambda b, c: (c, 0, 0))],
        out_specs=pl.BlockSpec((None, None, NSA_GROUPS, n, NSA_DH), lambda b, c: (b, c, 0, 0, 0)),
        out_shape=jax.ShapeDtypeStruct((bsz, 2, NSA_GROUPS, n, NSA_DH), F32),
        compiler_params=_cparams("parallel", "parallel"),
        name="nsa_compress",
    )(pieces, pe_ab, w1_cat, w2)


def _bias_softmax(s, bias):
    s = s + bias
    m = jnp.max(s, axis=-1, keepdims=True)
    p = jnp.exp(s - m)
    inv = jnp.where(m > 0.5 * NEG, 1.0 / jnp.sum(p, axis=-1, keepdims=True), 0.0)
    return p * inv


def _stack_heads(x):
    return jnp.concatenate([x] * NSA_REP, axis=0)


def _sum_heads(p, tq):
    out = p[0:tq]
    for r in range(1, NSA_REP):
        out = out + p[r * tq:(r + 1) * tq]
    return out


def _sel_scores(imp, blk, tp, ns):
    cur = tp // SEL_BLOCK
    forced = (blk == 0) | (blk == cur) | (blk == cur - 1)
    valid = blk * SEL_BLOCK <= tp
    score = jnp.where(forced, 1e6, jnp.where(valid, imp, -1e6))
    return jnp.where(blk < ns, score, -3e6)


def _top_n_mask(score, blk, axis, n):
    sel = jnp.zeros(score.shape, F32)
    for _ in range(n):
        mx = jnp.max(score, axis=axis, keepdims=True)
        first = jnp.min(jnp.where(score == mx, blk, LANES), axis=axis, keepdims=True)
        pick = blk == first
        sel = jnp.where(pick, 1.0, sel)
        score = jnp.where(pick, -3e38, score)
    return sel


def _gate_cols(gt, col0, br):
    return jnp.concatenate([gt[:, col0 + r * 3 + br:col0 + r * 3 + br + 1] for r in range(NSA_REP)], axis=0)


def _softmax_keys_on_rows(s):
    m = jnp.max(s, axis=0, keepdims=True)
    p = jnp.exp(s - m)
    inv = jnp.where(m > 0.5 * NEG, 1.0 / jnp.sum(p, axis=0, keepdims=True), 0.0)
    return p * inv


def _nsa_prompt_kernel(q_ref, gt_ref, kc_ref, vct_ref, ks_ref, vst_ref, kw_ref, vwt_ref, o_ref, bsel_scr,
                       *, tq, ncp, ns, ck, nw):
    rep, dh = NSA_REP, NSA_DH
    qb = pl.program_id(2) * tq
    n_chunks = (qb + tq - 1) // ck + 1
    ws = pl.multiple_of(jnp.maximum(qb - WINDOW, 0), tq)
    vb = vst_ref.shape[2]
    wbk = vwt_ref.shape[2]
    bpk = ck // SEL_BLOCK

    qt = (q_ref[...] * (dh ** -0.5)).T
    qt_cat = jnp.concatenate([qt[r * dh:(r + 1) * dh] for r in range(rep)], axis=1).astype(BF)

    def heads(x):
        return jnp.concatenate([x] * rep, axis=1)

    def tpos(shape):
        return lax.broadcasted_iota(jnp.int32, shape, 1) + qb

    n_idx = lax.broadcasted_iota(jnp.int32, (ncp, tq), 0)
    bias_c = jnp.where(CMP_STRIDE * n_idx + CMP_LEN - 1 <= tpos((ncp, tq)), 0.0, NEG)
    p_c = _softmax_keys_on_rows(_dot(kc_ref[...], qt_cat) + heads(bias_c))
    o_c = _dot(vct_ref[...], p_c.astype(BF))
    psum = p_c[:, 0:tq]
    for r in range(1, rep):
        psum = psum + p_c[:, r * tq:(r + 1) * tq]

    wpos = lax.broadcasted_iota(jnp.int32, (nw, tq), 0) + ws
    tw = tpos((nw, tq))
    bias_w = jnp.where((wpos <= tw) & (wpos > tw - WINDOW), 0.0, NEG)
    p_w = _softmax_keys_on_rows(_dot(kw_ref[pl.ds(ws, nw), :], qt_cat) + heads(bias_w)).astype(BF)
    wblk = ws // wbk
    o_w = _dot(vwt_ref[wblk], p_w[0:wbk])
    for jb in range(1, nw // wbk):
        o_w = o_w + _dot(vwt_ref[wblk + jb], p_w[jb * wbk:(jb + 1) * wbk])

    cj = lax.broadcasted_iota(jnp.int32, (LANES, ncp), 0) * SEL_BLOCK
    cn = lax.broadcasted_iota(jnp.int32, (LANES, ncp), 1) * CMP_STRIDE
    cover_t = jnp.where((cn <= cj + SEL_BLOCK - 1) & (cn + CMP_LEN - 1 >= cj), 1.0, 0.0).astype(BF)
    hi, mid, lo = _split3(psum)
    imp_t = _dot(cover_t, hi) + _dot(cover_t, mid) + _dot(cover_t, lo)
    blk = lax.broadcasted_iota(jnp.int32, (LANES, tq), 0)
    sel_t = _top_n_mask(_sel_scores(imp_t, blk, tpos((LANES, tq)), ns), blk, 0, min(SEL_TOPN, ns))
    bsel_scr[...] = jnp.where(sel_t > 0.5, 0.0, NEG)

    def sel_chunk(c, carry, causal):
        m, acc = carry
        s = _dot(ks_ref[pl.ds(pl.multiple_of(c * ck, ck), ck), :], qt_cat)
        brows = heads(bsel_scr[pl.ds(pl.multiple_of(c * bpk, bpk), bpk), :])
        s = jnp.concatenate([s[j * SEL_BLOCK:(j + 1) * SEL_BLOCK] + brows[j:j + 1] for j in range(bpk)], axis=0)
        if causal:
            kpos = lax.broadcasted_iota(jnp.int32, (ck, tq), 0) + c * ck
            s = s + heads(jnp.where(kpos <= tpos((ck, tq)), 0.0, NEG))
        m_new = jnp.maximum(m, jnp.max(s, axis=0, keepdims=True))
        p = jnp.exp(s - m_new).astype(BF)
        acc = jnp.exp(m - m_new) * acc
        for jb in range(ck // vb):
            acc = acc + _dot(vst_ref[c * (ck // vb) + jb], p[jb * vb:(jb + 1) * vb])
        return m_new, acc

    init = (jnp.full((1, rep * tq), NEG, F32), jnp.zeros((vst_ref.shape[1], rep * tq), F32))
    fin = lax.fori_loop(0, n_chunks - 1, functools.partial(sel_chunk, causal=False), init)
    _, acc_s = sel_chunk(n_chunks - 1, fin, True)
    o_s = acc_s[:dh] / jnp.maximum(acc_s[dh:dh + 1], 1e-30)

    gt = _sigmoid(gt_ref[...])

    def gate(br):
        return jnp.concatenate([gt[3 * r + br:3 * r + br + 1] for r in range(rep)], axis=1)

    o_all = gate(0) * o_c + gate(1) * o_s + gate(2) * o_w
    o_ref[...] = jnp.concatenate([o_all[:, r * tq:(r + 1) * tq] for r in range(rep)], axis=0).T


def nsa_attn_prompt(q, gates_t, kc, vct, ks, vst, kw, vwt, *, tq, ck):
    bsz, t_len, _ = q.shape
    ncp = kc.shape[2]
    gw = NSA_REP * NSA_DH
    ns = -(-t_len // SEL_BLOCK)
    vb, wbk = vst.shape[-1], vwt.shape[-1]
    assert tq == LANES and ck % vb == 0 and tq % wbk == 0 and (ck // SEL_BLOCK) % 8 == 0 and ck % tq == 0
    assert ns <= LANES and t_len % ck == 0 and t_len >= WINDOW + tq
    ones_rows = 16
    ones = jnp.concatenate([jnp.ones(vst.shape[:3] + (1, vb), BF),
                            jnp.zeros(vst.shape[:3] + (ones_rows - 1, vb), BF)], axis=3)
    vst = jnp.concatenate([vst, ones], axis=3)

    def blk_spec(x):
        return pl.BlockSpec((None, None) + x.shape[2:], lambda b, g, i: (b, g) + (0,) * (x.ndim - 2))

    return pl.pallas_call(
        functools.partial(_nsa_prompt_kernel, tq=tq, ncp=ncp, ns=ns, ck=ck, nw=WINDOW + tq),
        scratch_shapes=[pltpu.VMEM((LANES, tq), F32)],
        grid=(bsz, NSA_GROUPS, t_len // tq),
        in_specs=[pl.BlockSpec((None, tq, gw), lambda b, g, i: (b, i, g)),
                  pl.BlockSpec((None, None, NSA_REP * 3, tq), lambda b, g, i: (b, g, 0, i)),
                  blk_spec(kc), blk_spec(vct), blk_spec(ks), blk_spec(vst), blk_spec(kw), blk_spec(vwt)],
        out_specs=pl.BlockSpec((None, tq, gw), lambda b, g, i: (b, i, g)),
        out_shape=jax.ShapeDtypeStruct((bsz, t_len, NSA_GROUPS * gw), F32),
        compiler_params=_cparams("parallel", "parallel", "arbitrary"),
        name="nsa_attn_prompt",
    )(q, gates_t, kc, vct, ks, vst, kw, vwt)


def _nsa_sample_kernel(pt_ref, q_ref, gt_ref, kvn_ref, wn_ref, *rest, n_pages, tq, past_len, wb):
    del pt_ref
    cmp_refs, sel_refs = rest[:n_pages], rest[n_pages:2 * n_pages]
    win_ref, pe_ref, w1_ref, w2_ref, o_ref = rest[2 * n_pages:]
    rep, dh, ng = NSA_REP, NSA_DH, NSA_GROUPS
    rows = rep * tq
    seg = PAGE_SIZE
    ppp = PAGE_SIZE // CMP_STRIDE
    ncp = n_pages * ppp
    nk = past_len + seg
    ns = -(-(past_len + tq) // SEL_BLOCK)

    def tpos(shape):
        return (lax.broadcasted_iota(jnp.int32, shape, 0) & (tq - 1)) + past_len

    kcvc = []
    for c in range(2):
        pieces = jnp.concatenate([cmp_refs[p][c, g] for g in range(ng) for p in range(n_pages)], axis=0)
        a, b = _compress_halves(pieces, pe_ref[c], w1_ref[c])
        per_group = []
        for g in range(ng):
            h = _silu(a[g * ncp:(g + 1) * ncp] + pltpu.roll(b[g * ncp:(g + 1) * ncp], ncp - 1, 0))
            per_group.append(_dot(h.astype(BF), w2_ref[c]).astype(BF))
        kcvc.append(per_group)

    q = q_ref[...] * (dh ** -0.5)
    kvn = kvn_ref[...]
    wn = wn_ref[...]
    kvw = ng * dh

    def q4_of(g):
        return jnp.concatenate([q[:, (g * rep + r) * dh:(g * rep + r + 1) * dh] for r in range(rep)],
                               axis=0).astype(BF)

    def new_seg(x, col):
        return jnp.concatenate([x[:, col:col + dh], jnp.zeros((seg - tq, dh), F32)], axis=0).astype(BF)

    n_idx = lax.broadcasted_iota(jnp.int32, (rows, ncp), 1)
    bias_c = jnp.where(CMP_STRIDE * n_idx + CMP_LEN - 1 <= tpos((rows, ncp)), 0.0, NEG)
    q4s, o_cs, psums = [], [], []
    for g in range(ng):
        q4 = q4_of(g)
        p_c = _bias_softmax(_dot_nt(q4, kcvc[0][g]), bias_c)
        q4s.append(q4)
        o_cs.append(_dot(p_c.astype(BF), kcvc[1][g]))
        psums.append(_sum_heads(p_c, tq))
    cn = lax.broadcasted_iota(jnp.int32, (ncp, LANES), 0) * CMP_STRIDE
    cj = lax.broadcasted_iota(jnp.int32, (ncp, LANES), 1) * SEL_BLOCK
    cover = jnp.where((cn <= cj + SEL_BLOCK - 1) & (cn + CMP_LEN - 1 >= cj), 1.0, 0.0).astype(BF)
    hi, mid, lo = _split3(jnp.concatenate(psums, axis=0))
    imp = _dot(hi, cover) + _dot(mid, cover) + _dot(lo, cover)
    blk = lax.broadcasted_iota(jnp.int32, (ng * tq, LANES), 1)
    sel = _top_n_mask(_sel_scores(imp, blk, tpos((ng * tq, LANES)), ns), blk, 1, min(SEL_TOPN, ns))
    ej = lax.broadcasted_iota(jnp.int32, (LANES, nk), 0)
    ek = lax.broadcasted_iota(jnp.int32, (LANES, nk), 1)
    expand = jnp.where(ek // SEL_BLOCK == ej, 1.0, 0.0).astype(BF)
    sel_rows = jnp.concatenate([sel[g * tq:(g + 1) * tq] for g in range(ng) for _ in range(rep)], axis=0)
    mk_all = _dot(sel_rows.astype(BF), expand)
    kpos = lax.broadcasted_iota(jnp.int32, (rows, nk), 1)
    causal = kpos <= tpos((rows, nk))
    wp = lax.broadcasted_iota(jnp.int32, (rows, wb + seg), 1) + (past_len - wb)
    tw = tpos((rows, wb + seg))
    bias_w = jnp.where((wp <= tw) & (wp > tw - WINDOW), 0.0, NEG)

    gt = _sigmoid(gt_ref[...])
    for g in range(ng):
        q4 = q4s[g]
        sc = jnp.concatenate([_dot(q4, sel_refs[p][0, g]) for p in range(n_pages)]
                             + [_dot_nt(q4, new_seg(kvn, 2 * kvw + g * dh))], axis=1)
        bias = jnp.where(causal & (mk_all[g * rows:(g + 1) * rows] > 0.5), 0.0, NEG)
        p_s = _bias_softmax(sc, bias).astype(BF)
        o_s = _dot(p_s[:, past_len:], new_seg(kvn, 3 * kvw + g * dh))
        for p in range(n_pages):
            o_s = o_s + _dot_nt(p_s[:, p * seg:(p + 1) * seg], sel_refs[p][1, g])
        sw = jnp.concatenate([_dot(q4, win_ref[0, g]), _dot_nt(q4, new_seg(wn, g * dh))], axis=1)
        p_w = _bias_softmax(sw, bias_w).astype(BF)
        o_w = _dot_nt(p_w[:, :wb], win_ref[1, g]) + _dot(p_w[:, wb:], new_seg(wn, kvw + g * dh))
        c0 = g * rep * 3
        o4 = _gate_cols(gt, c0, 0) * o_cs[g] + _gate_cols(gt, c0, 1) * o_s + _gate_cols(gt, c0, 2) * o_w
        o_ref[:, g * rep * dh:(g + 1) * rep * dh] = jnp.concatenate(
            [o4[r * tq:(r + 1) * tq] for r in range(rep)], axis=1)


def nsa_attn_sample(q, gates, kv_new, win_new, cmp_pages, sel_pages, win_t, page_table, pe_ab, w1_cat, w2):
    bsz, tq, hq = q.shape
    n_pages = page_table.shape[1]
    wb = win_t.shape[-1]
    assert tq & (tq - 1) == 0 and tq <= PAGE_SIZE

    def row_spec(w):
        return pl.BlockSpec((None, tq, w), lambda b, pt: (b, 0, 0))

    def page_spec(shape, p):
        return pl.BlockSpec((None,) + shape, lambda b, pt: (pt[b, p],) + (0,) * len(shape))

    def full_spec(x):
        return pl.BlockSpec(x.shape, lambda b, pt: (0,) * x.ndim)

    in_specs = ([row_spec(hq), row_spec(gates.shape[2]), row_spec(kv_new.shape[2]), row_spec(win_new.shape[2])]
                + [page_spec(cmp_pages.shape[1:], p) for p in range(n_pages)]
                + [page_spec(sel_pages.shape[1:], p) for p in range(n_pages)]
                + [pl.BlockSpec((None,) + win_t.shape[1:], lambda b, pt: (b, 0, 0, 0, 0)),
                   full_spec(pe_ab), full_spec(w1_cat), full_spec(w2)])
    return pl.pallas_call(
        functools.partial(_nsa_sample_kernel, n_pages=n_pages, tq=tq, past_len=n_pages * PAGE_SIZE, wb=wb),
        grid_spec=pltpu.PrefetchScalarGridSpec(
            num_scalar_prefetch=1, grid=(bsz,), in_specs=in_specs, out_specs=row_spec(hq)),
        out_shape=jax.ShapeDtypeStruct((bsz, tq, hq), F32),
        compiler_params=_cparams("parallel"),
        name="nsa_attn_sample",
    )(page_table, q, gates, kv_new, win_new, *([cmp_pages] * n_pages), *([sel_pages] * n_pages),
      win_t, pe_ab, w1_cat, w2)


def _pad_cols(w, n):
    return jnp.pad(w, ((0, 0), (0, n - w.shape[1])))


def _group_major(x, bsz, t_len):
    return x.reshape(bsz, t_len, NSA_GROUPS, NSA_DH).transpose(0, 2, 1, 3)


def _group_major_t(x, bsz, t_len, blk):
    x5 = x.reshape(bsz, t_len // blk, blk, NSA_GROUPS, NSA_DH)
    return x5.transpose(0, 3, 1, 4, 2)


def _ffn_chunk(d_ff):
    n = d_ff // LANES
    for parts in range(2, n + 1):
        if n % parts == 0:
            return (n // parts) * LANES
    return d_ff


def _to_time_major(x, bsz, t_len):
    return x.reshape(bsz, t_len, -1).transpose(1, 0, 2).reshape(t_len * bsz, -1)


def _to_batch_major(x, bsz, t_len):
    return x.reshape(t_len, bsz, -1).transpose(1, 0, 2).reshape(bsz * t_len, -1)


def kernel(x_prompt, x_sample, cache_gla_state, cache_conv, cache_nsa_kv, cache_nsa_win, cache_mem_kv,
           cache_ffn_conv, page_table, mem_prompt, norm_mix, norm_mem, norm_x, norm_ffn, norm_final,
           w_in_a, w_gate_a, b_gate_a, g_gla_out, w_dw_b, b_dw_b, g_ln_b, b_ln_b, w_out_a,
           w_in_c, pe_cmp, w_cmp1, w_cmp2, w_out_c, w_xq, w_mem_kv, w_xo, w_up, w_ffn_dw, b_ffn_dw, w_down):
    bp, tp, d = x_prompt.shape
    bs, ts, _ = x_sample.shape
    depth = norm_mix.shape[0]
    n_mem = mem_prompt.shape[1]
    d_ff = w_down.shape[1]
    hk, hv = GLA_HEADS * GLA_DK, GLA_HEADS * GLA_DV
    kvw = NSA_GROUPS * NSA_DH
    hq = NSA_HEADS * NSA_DH

    xp = x_prompt.reshape(bp * tp, d)
    xs = x_sample.reshape(bs * ts, d)
    tm_p = 512
    tm_s = bs * ts

    outs = {k: [] for k in ("gla_p", "gla_s", "conv_p", "conv_s", "nsa_p", "nsa_s", "win_p", "win_s",
                            "mem_p", "ffn_p", "ffn_s")}
    for l in range(depth):
        i = l // 2
        if l % 2 == 0:
            wi = w_in_a[i]
            c0, c1, c2, c3, c4 = hk, 2 * hk, 2 * hk + hv, 2 * hk + 2 * hv, 2 * hk + 2 * hv + GLA_RANK
            w_a = jnp.concatenate([wi[:, c4:], wi[:, c1:c2], wi[:, c2:c3], wi[:, :c0], wi[:, c0:c1],
                                   _pad_cols(wi[:, c3:c4], LANES)], axis=1).astype(BF)
            splits = (2 * CONV_CH, hv, hv, hk, hk, LANES)
            wg = jnp.pad(w_gate_a[i], ((0, LANES - GLA_RANK), (0, 0))).astype(BF)
            bg = b_gate_a[i].reshape(1, hk)
            go = g_gla_out[i].reshape(1, hv)
            w_out = w_out_a[i].astype(BF)
            vecs = [v.reshape(1, CONV_CH) for v in (b_dw_b[i], g_ln_b[i], b_ln_b[i])]

            def mixer(x, bsz, t_len, tm, s0, hist_tm, n_outer, ctm, step):
                u, v, r, q, k, a = norm_matmul(x, norm_mix[l], w_a, splits, tm)
                sh = lambda z: z.reshape(bsz, t_len, -1)
                og, s_new = gla(sh(q), sh(k), sh(v), sh(r), sh(a), wg, bg, go, s0, 1024)
                if step > 1:
                    u = _to_time_major(u, bsz, t_len)
                c, hist_new = convmod(u, hist_tm, w_dw_b[i], *vecs, n_outer, ctm, step)
                if step > 1:
                    c = _to_batch_major(c, bsz, t_len)
                y = matmul_res([og.reshape(bsz * t_len, hv), c], [w_out[:hv], w_out[hv:]], x, tm)
                return y, s_new, hist_new

            xp, sp, cp = mixer(xp, bp, tp, tm_p, jnp.zeros((bp, GLA_HEADS, GLA_DK, GLA_DV), F32),
                               jnp.zeros((bp, CONV_WIDTH - 1, CONV_CH), F32), bp, 512, 1)
            hist_s = cache_conv[i].transpose(1, 0, 2).reshape(1, (CONV_WIDTH - 1) * bs, CONV_CH)
            xs, ss, cs = mixer(xs, bs, ts, tm_s, cache_gla_state[i], hist_s, 1, bs * ts, bs)
            cs = cs.reshape(CONV_WIDTH - 1, bs, CONV_CH).transpose(1, 0, 2)
            outs["gla_p"].append(sp)
            outs["gla_s"].append(ss)
            outs["conv_p"].append(cp)
            outs["conv_s"].append(cs)
        else:
            n_in = hq + 6 * kvw + 3 * NSA_HEADS
            n_pad = -(-n_in // LANES) * LANES
            w_c = _pad_cols(w_in_c[i], n_pad).astype(BF)
            splits = (hq, 4 * kvw, 2 * kvw, n_pad - hq - 6 * kvw)
            w_out = w_out_c[i].astype(BF)
            pw = CMP_STRIDE * NSA_DH
            pe_ab = pe_cmp[i].reshape(2, 2, pw)
            w1_ab = w_cmp1[i].reshape(2, 2, pw, CMP_HIDDEN)
            w1_cat = jnp.concatenate([w1_ab[:, 0], w1_ab[:, 1]], axis=-1).astype(BF)
            w2 = w_cmp2[i].astype(BF)

            def pieces_of(rows, bsz, n):
                r6 = rows[:, :n * CMP_STRIDE].reshape(bsz, n, CMP_STRIDE, 2, NSA_GROUPS, NSA_DH)
                return r6.transpose(0, 3, 4, 1, 2, 5).reshape(bsz, 2, NSA_GROUPS, n, pw)

            q, kv4, win, gl = norm_matmul(xp, norm_mix[l], w_c, splits, tm_p)
            kv4b = kv4.reshape(bp, tp, 4 * kvw)
            kcvc = compress(pieces_of(kv4b[:, :, :2 * kvw], bp, tp // CMP_STRIDE), pe_ab, w1_cat, w2)
            ks = _group_major(kv4[:, 2 * kvw:3 * kvw], bp, tp).astype(BF)
            vst = _group_major_t(kv4[:, 3 * kvw:], bp, tp, 2 * LANES).astype(BF)
            kw = _group_major(win[:, :kvw], bp, tp).astype(BF)
            vwt = _group_major_t(win[:, kvw:], bp, tp, LANES).astype(BF)
            gates_t = gl[:, :3 * NSA_HEADS].reshape(bp, tp, NSA_GROUPS, NSA_REP * 3).transpose(0, 2, 3, 1)
            o = nsa_attn_prompt(q.reshape(bp, tp, hq), gates_t, kcvc[:, 0].astype(BF),
                                kcvc[:, 1].transpose(0, 1, 3, 2).astype(BF), ks, vst, kw, vwt,
                                tq=Q_BLOCK, ck=1024)
            xp = matmul_res([o.reshape(bp * tp, hq)], [w_out], xp, tm_p)
            win_rows = min(WINDOW, tp)
            outs["nsa_p"].append(kv4.reshape(bp, tp, 4, NSA_GROUPS, NSA_DH))
            outs["win_p"].append(win.reshape(bp, tp, 2, NSA_GROUPS, NSA_DH)[:, tp - win_rows:])

            q, kv4, win, gl = norm_matmul(xs, norm_mix[l], w_c, splits, tm_s)
            pool = cache_nsa_kv[i]
            n_phys = pool.shape[0]
            ppp = PAGE_SIZE // CMP_STRIDE
            cmp_pages = (pool[:, :, :2].reshape(n_phys, ppp, CMP_STRIDE, 2, NSA_GROUPS, NSA_DH)
                         .transpose(0, 3, 4, 1, 2, 5).reshape(n_phys, 2, NSA_GROUPS, ppp, pw))
            sel_pages = pool[:, :, 2:].transpose(0, 2, 3, 4, 1).astype(BF)
            win_cache = cache_nsa_win[i]
            wb = win_cache.shape[1]
            win_t = win_cache.transpose(0, 2, 3, 4, 1).astype(BF)
            o = nsa_attn_sample(q.reshape(bs, ts, hq), gl.reshape(bs, ts, -1), kv4.reshape(bs, ts, 4 * kvw),
                                win.reshape(bs, ts, 2 * kvw), cmp_pages, sel_pages, win_t, page_table,
                                pe_ab, w1_cat, w2)
            xs = matmul_res([o.reshape(bs * ts, hq)], [w_out], xs, tm_s)
            outs["nsa_s"].append(kv4.reshape(bs, ts, 4, NSA_GROUPS, NSA_DH))
            win_new = win.reshape(bs, ts, 2, NSA_GROUPS, NSA_DH)
            outs["win_s"].append(jnp.concatenate([win_cache, win_new], axis=1)[:, ts:])

        xhw = X_HEADS * X_DH
        (mkv,) = norm_matmul(mem_prompt.reshape(bp * n_mem, d), norm_mem[l], w_mem_kv[l].astype(BF),
                             (2 * xhw,), bp * n_mem)
        outs["mem_p"].append(mkv.reshape(bp, n_mem, 2, X_HEADS, X_DH))
        w_q = w_xq[l].astype(BF)
        w_o = w_xo[l].astype(BF)
        (qx,) = norm_matmul(xp, norm_x[l], w_q, (xhw,), tm_p)
        ox = xattn(qx.reshape(bp, tp, xhw), mkv.reshape(1, bp, n_mem, 2, X_HEADS, X_DH), 0, 512)
        xp = matmul_res([ox.reshape(bp * tp, xhw)], [w_o], xp, tm_p)
        (qx,) = norm_matmul(xs, norm_x[l], w_q, (xhw,), tm_s)
        ox = xattn(qx.reshape(bs, ts, xhw), cache_mem_kv, l, ts)
        xs = matmul_res([ox.reshape(bs * ts, xhw)], [w_o], xs, tm_s)

        wu = w_up[l].astype(BF)
        wd = w_down[l].astype(BF)
        fg = norm_final if l == depth - 1 else None
        cw = _ffn_chunk(d_ff)
        xp, hfp = conv_ffn(xp, norm_ffn[l], jnp.zeros((bp, FFN_WIDTH - 1, 2 * d_ff), F32), wu, w_ffn_dw[l],
                           b_ffn_dw[l], wd, bp, 512, 1, cw, fg)
        hist_s = cache_ffn_conv[l].transpose(1, 0, 2).reshape(1, (FFN_WIDTH - 1) * bs, 2 * d_ff)
        xs_tm, hfs = conv_ffn(_to_time_major(xs, bs, ts), norm_ffn[l], hist_s, wu, w_ffn_dw[l], b_ffn_dw[l], wd,
                              1, bs * ts, bs, cw, fg)
        xs = _to_batch_major(xs_tm, bs, ts)
        outs["ffn_p"].append(hfp)
        outs["ffn_s"].append(hfs.reshape(FFN_WIDTH - 1, bs, 2 * d_ff).transpose(1, 0, 2))

    st = jnp.stack
    return (xp.reshape(bp, tp, d), xs.reshape(bs, ts, d),
            st(outs["gla_p"]), st(outs["gla_s"]), st(outs["conv_p"]), st(outs["conv_s"]),
            st(outs["nsa_p"]), st(outs["nsa_s"]), st(outs["win_p"]), st(outs["win_s"]),
            st(outs["mem_p"]), st(outs["ffn_p"]), st(outs["ffn_s"]))
```

```python
import functools
import math

import jax
import jax.numpy as jnp
from jax import lax
from jax.experimental import pallas as pl
from jax.experimental.pallas import tpu as pltpu

EPS = 1e-6
NEG = -1e30
F32 = jnp.float32
BF = jnp.bfloat16

V7X_VMEM_BYTES = 64 * 1024 * 1024
VMEM_LIMIT = V7X_VMEM_BYTES - 8 * 1024 * 1024
LANES = 128

GLA_HEADS, GLA_DK, GLA_DV, GLA_RANK, GLA_TAU, GLA_CHUNK = 4, 64, 128, 16, 16.0, 16
GLA_UNROLL = 4
CONV_CH, CONV_WIDTH = 512, 31
NSA_HEADS, NSA_GROUPS, NSA_DH = 16, 4, 64
NSA_REP = NSA_HEADS // NSA_GROUPS
CMP_LEN, CMP_STRIDE, CMP_HIDDEN = 32, 16, 64
SEL_BLOCK, SEL_TOPN, WINDOW, Q_BLOCK = 64, 16, 512, 128
X_HEADS, X_DH = 4, 128
FFN_WIDTH = 3
PAGE_SIZE = 128


def _cparams(*sem):
    return pltpu.CompilerParams(dimension_semantics=sem, vmem_limit_bytes=VMEM_LIMIT)


def _rms(x, g):
    return x * lax.rsqrt(jnp.mean(x * x, axis=-1, keepdims=True) + EPS) * g


def _sigmoid(x):
    return 1.0 / (1.0 + jnp.exp(-x))


def _silu(x):
    return x * _sigmoid(x)


def _dot(a, b):
    return jnp.dot(a, b, preferred_element_type=F32)


def _dot_nt(a, b):
    return lax.dot_general(a, b, (((1,), (1,)), ((), ())), preferred_element_type=F32)


def _dot_tn(a, b):
    return lax.dot_general(a, b, (((0,), (0,)), ((), ())), preferred_element_type=F32)


def _split3(x):
    hi = x.astype(BF)
    r1 = x - hi.astype(F32)
    mid = r1.astype(BF)
    lo = (r1 - mid.astype(F32)).astype(BF)
    return hi, mid, lo


def _norm_matmul_kernel(x_ref, g_ref, w_ref, *o_refs, splits):
    xn = _rms(x_ref[...], g_ref[...]).astype(BF)
    off = 0
    for o_ref, n in zip(o_refs, splits):
        o_ref[...] = _dot(xn, w_ref[:, off:off + n])
        off += n


def norm_matmul(x, g, w, splits, tm):
    m, d = x.shape
    n = w.shape[1]
    assert sum(splits) == n and m % tm == 0
    outs = pl.pallas_call(
        functools.partial(_norm_matmul_kernel, splits=tuple(splits)),
        grid=(m // tm,),
        in_specs=[pl.BlockSpec((tm, d), lambda i: (i, 0)),
                  pl.BlockSpec((1, d), lambda i: (0, 0)),
                  pl.BlockSpec((d, n), lambda i: (0, 0))],
        out_specs=[pl.BlockSpec((tm, s), lambda i: (i, 0)) for s in splits],
        out_shape=[jax.ShapeDtypeStruct((m, s), F32) for s in splits],
        compiler_params=_cparams("parallel"),
        name="norm_matmul",
    )(x, g.reshape(1, d), w)
    return outs


def _matmul_res_kernel(*refs, n_in):
    a_refs, w_refs = refs[:n_in], refs[n_in:2 * n_in]
    res_ref, o_ref = refs[2 * n_in], refs[2 * n_in + 1]
    acc = _dot(a_refs[0][...].astype(BF), w_refs[0][...])
    for a_ref, w_ref in zip(a_refs[1:], w_refs[1:]):
        acc = acc + _dot(a_ref[...].astype(BF), w_ref[...])
    o_ref[...] = res_ref[...] + acc


def matmul_res(a_list, w_list, res, tm):
    m, d = res.shape
    n_in = len(a_list)
    return pl.pallas_call(
        functools.partial(_matmul_res_kernel, n_in=n_in),
        grid=(m // tm,),
        in_specs=([pl.BlockSpec((tm, a.shape[1]), lambda i: (i, 0)) for a in a_list]
                  + [pl.BlockSpec(w.shape, lambda i: (0, 0)) for w in w_list]
                  + [pl.BlockSpec((tm, d), lambda i: (i, 0))]),
        out_specs=pl.BlockSpec((tm, d), lambda i: (i, 0)),
        out_shape=jax.ShapeDtypeStruct((m, d), F32),
        compiler_params=_cparams("parallel"),
        name="matmul_res",
    )(*a_list, *w_list, res)


def _gla_kernel(q_ref, k_ref, v_ref, r_ref, a_ref, wg_ref, bg_ref, go_ref, s0_ref,
                o_ref, sout_ref, s_scr, b_scr, *, chunk, n_chunks):
    j = pl.program_id(1)
    tt = chunk * n_chunks
    blk = min(LANES, tt)

    @pl.when(j == 0)
    def _():
        s_scr[...] = s0_ref[...]

    ri = lax.broadcasted_iota(jnp.int32, (chunk, chunk), 0)
    ci = lax.broadcasted_iota(jnp.int32, (chunk, chunk), 1)
    tri = (ri >= ci).astype(F32)
    eye = (lax.broadcasted_iota(jnp.int32, (GLA_DK, GLA_DK), 0)
           == lax.broadcasted_iota(jnp.int32, (GLA_DK, GLA_DK), 1)).astype(F32)

    rb = lax.broadcasted_iota(jnp.int32, (blk, blk), 0)
    cb = lax.broadcasted_iota(jnp.int32, (blk, blk), 1)
    tri_blk = jnp.where((rb >= cb) & (rb // chunk == cb // chunk), 1.0, 0.0).astype(BF)
    for nb in range(tt // blk):
        rs = slice(nb * blk, (nb + 1) * blk)
        z = _dot(a_ref[rs, :].astype(BF), wg_ref[...]) + bg_ref[...]
        log_a = (jnp.minimum(z, 0.0) - jnp.log(1.0 + jnp.exp(-jnp.abs(z)))) / GLA_TAU
        hi, mid, lo = _split3(log_a)
        b_scr[rs, :] = _dot(tri_blk, hi) + _dot(tri_blk, mid) + _dot(tri_blk, lo)

    def body(i, carry):
        sl = pl.ds(pl.multiple_of(i * chunk, chunk), chunk)
        qc = q_ref[sl, :] * (GLA_DK ** -0.5)
        kc = k_ref[sl, :]
        b = b_scr[sl, :]
        new_s = []
        for h in range(GLA_HEADS):
            ks = slice(h * GLA_DK, (h + 1) * GLA_DK)
            vs = slice(h * GLA_DV, (h + 1) * GLA_DV)
            bh = b[:, ks]
            b_last = bh[chunk - 1:chunk, :]
            qe = (qc[:, ks] * jnp.exp(bh)).astype(BF)
            ke = (kc[:, ks] * jnp.exp(-bh)).astype(BF)
            kl = (kc[:, ks] * jnp.exp(b_last - bh)).astype(BF)
            vh = v_ref[sl, vs].astype(BF)
            s = carry[h]
            att = _dot_nt(qe, ke) * tri
            o = _dot(att.astype(BF), vh) + _dot(qe, s.astype(BF))
            decay = jnp.exp(jnp.sum(eye * b_last, axis=1, keepdims=True))
            new_s.append(decay * s + _dot_tn(kl, vh))
            o = o * lax.rsqrt(jnp.mean(o * o, axis=-1, keepdims=True) + EPS) * go_ref[:, vs]
            o_ref[sl, vs] = o * _silu(r_ref[sl, vs])
        return tuple(new_s)

    s_fin = lax.fori_loop(0, n_chunks, body, tuple(s_scr[h] for h in range(GLA_HEADS)),
                          unroll=math.gcd(n_chunks, GLA_UNROLL))
    for h in range(GLA_HEADS):
        s_scr[h] = s_fin[h]
    sout_ref[...] = s_scr[...]


def gla(q, k, v, r, a, wg, bg, go, s0, tt):
    bsz, t_len, _ = q.shape
    chunk = math.gcd(t_len, GLA_CHUNK)
    tt = min(tt, t_len)
    hk, hv = GLA_HEADS * GLA_DK, GLA_HEADS * GLA_DV

    def tspec(w):
        return pl.BlockSpec((None, tt, w), lambda b, j: (b, j, 0))

    sspec = pl.BlockSpec((None, GLA_HEADS, GLA_DK, GLA_DV), lambda b, j: (b, 0, 0, 0))
    return pl.pallas_call(
        functools.partial(_gla_kernel, chunk=chunk, n_chunks=tt // chunk),
        grid=(bsz, t_len // tt),
        in_specs=[tspec(hk), tspec(hk), tspec(hv), tspec(hv), tspec(LANES),
                  pl.BlockSpec((LANES, hk), lambda b, j: (0, 0)),
                  pl.BlockSpec((1, hk), lambda b, j: (0, 0)),
                  pl.BlockSpec((1, hv), lambda b, j: (0, 0)),
                  sspec],
        out_specs=[tspec(hv), sspec],
        out_shape=[jax.ShapeDtypeStruct((bsz, t_len, hv), F32),
                   jax.ShapeDtypeStruct((bsz, GLA_HEADS, GLA_DK, GLA_DV), F32)],
        scratch_shapes=[pltpu.VMEM((GLA_HEADS, GLA_DK, GLA_DV), F32), pltpu.VMEM((tt, hk), F32)],
        compiler_params=_cparams("parallel", "arbitrary"),
        name="gla",
    )(q, k, v, r, a, wg, bg, go, s0)


def _convmod_kernel(u1_ref, u2_ref, hist_ref, w_ref, b_ref, g_ref, bl_ref, c_ref, hout_ref, gpad,
                    *, tm, step, off):
    hrows = (CONV_WIDTH - 1) * step
    base = off - hrows

    @pl.when(pl.program_id(1) == 0)
    def _():
        gpad[base:off, :] = hist_ref[...]

    gpad[off:off + tm, :] = u1_ref[...] * _sigmoid(u2_ref[...])
    acc = b_ref[...] + w_ref[0:1, :] * gpad[base:base + tm, :]
    for kk in range(1, CONV_WIDTH):
        acc = acc + w_ref[kk:kk + 1, :] * gpad[base + kk * step:base + kk * step + tm, :]
    d = acc - jnp.mean(acc, axis=-1, keepdims=True)
    c = d * lax.rsqrt(jnp.mean(d * d, axis=-1, keepdims=True) + EPS) * g_ref[...] + bl_ref[...]
    c_ref[...] = _silu(c)
    new_hist = gpad[base + tm:off + tm, :]
    hout_ref[...] = new_hist
    gpad[base:off, :] = new_hist


def convmod(u, hist, w, b, g, bl, n_outer, tm, step):
    m = u.shape[0]
    rows = m // n_outer
    nt = rows // tm
    hrows = (CONV_WIDTH - 1) * step
    assert tm >= hrows or nt == 1
    off = -(-hrows // 8) * 8
    ch = CONV_CH
    vec = pl.BlockSpec((1, ch), lambda o, t: (0, 0))
    return pl.pallas_call(
        functools.partial(_convmod_kernel, tm=tm, step=step, off=off),
        grid=(n_outer, nt),
        in_specs=[pl.BlockSpec((tm, ch), lambda o, t: (o * nt + t, 0)),
                  pl.BlockSpec((tm, ch), lambda o, t: (o * nt + t, 1)),
                  pl.BlockSpec((None, hrows, ch), lambda o, t: (o, 0, 0)),
                  pl.BlockSpec((CONV_WIDTH, ch), lambda o, t: (0, 0)),
                  vec, vec, vec],
        out_specs=[pl.BlockSpec((tm, ch), lambda o, t: (o * nt + t, 0)),
                   pl.BlockSpec((None, hrows, ch), lambda o, t: (o, 0, 0))],
        out_shape=[jax.ShapeDtypeStruct((m, ch), F32),
                   jax.ShapeDtypeStruct((n_outer, hrows, ch), F32)],
        scratch_shapes=[pltpu.VMEM((off + tm, ch), F32)],
        compiler_params=_cparams("parallel", "arbitrary"),
        name="convmod",
    )(u, u, hist, w, b, g, bl)


def _xattn_kernel(q_ref, kv_ref, o_ref):
    q = q_ref[...] * (X_DH ** -0.5)
    for h in range(X_HEADS):
        sl = slice(h * X_DH, (h + 1) * X_DH)
        kh = kv_ref[:, 0, h, :].astype(BF)
        vh = kv_ref[:, 1, h, :].astype(BF)
        s = _dot_nt(q[:, sl].astype(BF), kh)
        p = jnp.exp(s - jnp.max(s, axis=-1, keepdims=True))
        p = p / jnp.sum(p, axis=-1, keepdims=True)
        o_ref[:, sl] = _dot(p.astype(BF), vh)


def xattn(q, kv, layer, tq):
    bsz, t_len, hw = q.shape
    n_mem = kv.shape[2]
    return pl.pallas_call(
        _xattn_kernel,
        grid=(bsz, t_len // tq),
        in_specs=[pl.BlockSpec((None, tq, hw), lambda b, i: (b, i, 0)),
                  pl.BlockSpec((None, None, n_mem, 2, X_HEADS, X_DH), lambda b, i: (layer, b, 0, 0, 0, 0))],
        out_specs=pl.BlockSpec((None, tq, hw), lambda b, i: (b, i, 0)),
        out_shape=jax.ShapeDtypeStruct((bsz, t_len, hw), F32),
        compiler_params=_cparams("parallel", "parallel"),
        name="xattn",
    )(q, kv)


def _ffn_kernel(*refs, tm, step, final_norm):
    (x_ref, g_ref, ha_ref, hb_ref, wua_ref, wub_ref, wda_ref, wdb_ref, ba_ref, bb_ref, wdn_ref) = refs[:11]
    rest = refs[11:]
    if final_norm:
        gf_ref, rest = rest[0], rest[1:]
    o_ref, sa_ref, sb_ref, xn_scr, acc_scr, halo_a, halo_b = rest
    t = pl.program_id(1)
    j = pl.program_id(2)
    hrows = (FFN_WIDTH - 1) * step

    @pl.when(j == 0)
    def _():
        xn_scr[...] = _rms(x_ref[...], g_ref[...]).astype(BF)

    @pl.when(t == 0)
    def _():
        halo_a[j] = ha_ref[...]
        halo_b[j] = hb_ref[...]

    def conv_half(wu_ref, wd_ref, b_ref, halo, s_ref):
        u = _dot(xn_scr[...], wu_ref[...])
        h = halo[j]
        if step % 8 == 0:
            u1 = jnp.concatenate([h[step:], u[:tm - step]], axis=0)
            u2 = jnp.concatenate([h, u[:tm - 2 * step]], axis=0)
        else:
            assert step == 1
            row = lax.broadcasted_iota(jnp.int32, (8, u.shape[1]), 0)
            r1, r2 = pltpu.roll(u, 1, 0), pltpu.roll(u, 2, 0)
            top1 = jnp.where(row == 0, h[1:2, :], r1[0:8])
            top2 = jnp.where(row == 0, h[0:1, :], jnp.where(row == 1, h[1:2, :], r2[0:8]))
            u1 = jnp.concatenate([top1, r1[8:]], axis=0)
            u2 = jnp.concatenate([top2, r2[8:]], axis=0)
        c = b_ref[...] + wd_ref[0:1, :] * u2 + wd_ref[1:2, :] * u1 + wd_ref[2:3, :] * u
        new_h = u[tm - hrows:, :]
        halo[j] = new_h
        s_ref[j] = new_h
        return c

    ca = conv_half(wua_ref, wda_ref, ba_ref, halo_a, sa_ref)
    cb = conv_half(wub_ref, wdb_ref, bb_ref, halo_b, sb_ref)
    contrib = _dot((_silu(ca) * cb).astype(BF), wdn_ref[...])

    @pl.when(j == 0)
    def _():
        acc_scr[...] = x_ref[...] + contrib

    @pl.when(j > 0)
    def _():
        acc_scr[...] = acc_scr[...] + contrib

    @pl.when(j == pl.num_programs(2) - 1)
    def _():
        if final_norm:
            o_ref[...] = _rms(acc_scr[...], gf_ref[...])
        else:
            o_ref[...] = acc_scr[...]


def conv_ffn(x, g, hist, w_up, w_dw, b_dw, w_down, n_outer, tm, step, cw, final_g=None):
    m, d = x.shape
    d_ff = w_down.shape[0]
    rows = m // n_outer
    nt = rows // tm
    nj = d_ff // cw
    hrows = (FFN_WIDTH - 1) * step
    assert tm >= hrows and d_ff % cw == 0
    final_norm = final_g is not None

    in_specs = [
        pl.BlockSpec((tm, d), lambda o, t, j: (o * nt + t, 0)),
        pl.BlockSpec((1, d), lambda o, t, j: (0, 0)),
        pl.BlockSpec((None, hrows, cw), lambda o, t, j: (o, 0, j)),
        pl.BlockSpec((None, hrows, cw), lambda o, t, j: (o, 0, nj + j)),
        pl.BlockSpec((d, cw), lambda o, t, j: (0, j)),
        pl.BlockSpec((d, cw), lambda o, t, j: (0, nj + j)),
        pl.BlockSpec((FFN_WIDTH, cw), lambda o, t, j: (0, j)),
        pl.BlockSpec((FFN_WIDTH, cw), lambda o, t, j: (0, nj + j)),
        pl.BlockSpec((1, cw), lambda o, t, j: (0, j)),
        pl.BlockSpec((1, cw), lambda o, t, j: (0, nj + j)),
        pl.BlockSpec((cw, d), lambda o, t, j: (j, 0)),
    ]
    args = [x, g.reshape(1, d), hist, hist, w_up, w_up, w_dw, w_dw, b_dw.reshape(1, -1), b_dw.reshape(1, -1), w_down]
    if final_norm:
        in_specs.append(pl.BlockSpec((1, d), lambda o, t, j: (0, 0)))
        args.append(final_g.reshape(1, d))
    out, sa, sb = pl.pallas_call(
        functools.partial(_ffn_kernel, tm=tm, step=step, final_norm=final_norm),
        grid=(n_outer, nt, nj),
        in_specs=in_specs,
        out_specs=[pl.BlockSpec((tm, d), lambda o, t, j: (o * nt + t, 0)),
                   pl.BlockSpec((None, nj, hrows, cw), lambda o, t, j: (o, 0, 0, 0)),
                   pl.BlockSpec((None, nj, hrows, cw), lambda o, t, j: (o, 0, 0, 0))],
        out_shape=[jax.ShapeDtypeStruct((m, d), F32),
                   jax.ShapeDtypeStruct((n_outer, nj, hrows, cw), F32),
                   jax.ShapeDtypeStruct((n_outer, nj, hrows, cw), F32)],
        scratch_shapes=[pltpu.VMEM((tm, d), BF), pltpu.VMEM((tm, d), F32),
                        pltpu.VMEM((nj, hrows, cw), F32), pltpu.VMEM((nj, hrows, cw), F32)],
        compiler_params=_cparams("parallel", "arbitrary", "arbitrary"),
        name="conv_ffn",
    )(*args)
    def cols(z):
        return z.transpose(0, 2, 1, 3).reshape(n_outer, hrows, d_ff)

    return out, jnp.concatenate([cols(sa), cols(sb)], axis=-1)


def _compress_halves(x, pe2, w1cat):
    pw = _dot(pe2.astype(BF), w1cat)
    const = jnp.concatenate([pw[0:1, :CMP_HIDDEN], pw[1:2, CMP_HIDDEN:]], axis=1)
    ab = _dot(x.astype(BF), w1cat) + const
    return ab[:, :CMP_HIDDEN], ab[:, CMP_HIDDEN:]


def _compress_kernel(p_ref, pe_ref, w1_ref, w2_ref, o_ref, *, n):
    row = lax.broadcasted_iota(jnp.int32, (n, NSA_DH), 0)
    for g in range(NSA_GROUPS):
        a, b = _compress_halves(p_ref[g], pe_ref[...], w1_ref[...])
        h = _silu(a + pltpu.roll(b, n - 1, 0))
        o = _dot(h.astype(BF), w2_ref[...])
        o_ref[g] = jnp.where(row < n - 1, o, 0.0)


def compress(pieces, pe_ab, w1_cat, w2):
    bsz, _, _, n, pw = pieces.shape
    return pl.pallas_call(
        functools.partial(_compress_kernel, n=n),
        grid=(bsz, 2),
        in_specs=[pl.BlockSpec((None, None, NSA_GROUPS, n, pw), lambda b, c: (b, c, 0, 0, 0)),
                  pl.BlockSpec((None, 2, pw), lambda b, c: (c, 0, 0)),
                  pl.BlockSpec((None, pw, 2 * CMP_HIDDEN), lambda b, c: (c, 0, 0)),
                  pl.BlockSpec((None, CMP_HIDDEN, NSA_DH), lambda b, c: (c, 0, 0))],
        out_specs=pl.BlockSpec((None, None, NSA_GROUPS, n, NSA_DH), lambda b, c: (b, c, 0, 0, 0)),
        out_shape=jax.ShapeDtypeStruct((bsz, 2, NSA_GROUPS, n, NSA_DH), F32),
        compiler_params=_cparams("parallel", "parallel"),
        name="nsa_compress",
    )(pieces, pe_ab, w1_cat, w2)


def _bias_softmax(s, bias):
    s = s + bias
    m = jnp.max(s, axis=-1, keepdims=True)
    p = jnp.exp(s - m)
    inv = jnp.where(m > 0.5 * NEG, 1.0 / jnp.sum(p, axis=-1, keepdims=True), 0.0)
    return p * inv


def _stack_heads(x):
    return jnp.concatenate([x] * NSA_REP, axis=0)


def _sum_heads(p, tq):
    out = p[0:tq]
    for r in range(1, NSA_REP):
        out = out + p[r * tq:(r + 1) * tq]
    return out


def _sel_scores(imp, blk, tp, ns):
    cur = tp // SEL_BLOCK
    forced = (blk == 0) | (blk == cur) | (blk == cur - 1)
    valid = blk * SEL_BLOCK <= tp
    score = jnp.where(forced, 1e6, jnp.where(valid, imp, -1e6))
    return jnp.where(blk < ns, score, -3e6)


def _top_n_mask(score, blk, axis, n):
    sel = jnp.zeros(score.shape, F32)
    for _ in range(n):
        mx = jnp.max(score, axis=axis, keepdims=True)
        first = jnp.min(jnp.where(score == mx, blk, LANES), axis=axis, keepdims=True)
        pick = blk == first
        sel = jnp.where(pick, 1.0, sel)
        score = jnp.where(pick, -3e38, score)
    return sel


def _gate_cols(gt, col0, br):
    return jnp.concatenate([gt[:, col0 + r * 3 + br:col0 + r * 3 + br + 1] for r in range(NSA_REP)], axis=0)


def _softmax_keys_on_rows(s):
    m = jnp.max(s, axis=0, keepdims=True)
    p = jnp.exp(s - m)
    inv = jnp.where(m > 0.5 * NEG, 1.0 / jnp.sum(p, axis=0, keepdims=True), 0.0)
    return p * inv


def _nsa_prompt_kernel(q_ref, gt_ref, kc_ref, vct_ref, ks_ref, vst_ref, kw_ref, vwt_ref, o_ref, bsel_scr,
                       *, tq, ncp, ns, ck, nw):
    rep, dh = NSA_REP, NSA_DH
    qb = pl.program_id(2) * tq
    n_chunks = (qb + tq - 1) // ck + 1
    ws = pl.multiple_of(jnp.maximum(qb - WINDOW, 0), tq)
    vb = vst_ref.shape[2]
    wbk = vwt_ref.shape[2]
    bpk = ck // SEL_BLOCK

    qt = (q_ref[...] * (dh ** -0.5)).T
    qt_cat = jnp.concatenate([qt[r * dh:(r + 1) * dh] for r in range(rep)], axis=1).astype(BF)

    def heads(x):
        return jnp.concatenate([x] * rep, axis=1)

    def tpos(shape):
        return lax.broadcasted_iota(jnp.int32, shape, 1) + qb

    n_idx = lax.broadcasted_iota(jnp.int32, (ncp, tq), 0)
    bias_c = jnp.where(CMP_STRIDE * n_idx + CMP_LEN - 1 <= tpos((ncp, tq)), 0.0, NEG)
    p_c = _softmax_keys_on_rows(_dot(kc_ref[...], qt_cat) + heads(bias_c))
    o_c = _dot(vct_ref[...], p_c.astype(BF))
    psum = p_c[:, 0:tq]
    for r in range(1, rep):
        psum = psum + p_c[:, r * tq:(r + 1) * tq]

    wpos = lax.broadcasted_iota(jnp.int32, (nw, tq), 0) + ws
    tw = tpos((nw, tq))
    bias_w = jnp.where((wpos <= tw) & (wpos > tw - WINDOW), 0.0, NEG)
    p_w = _softmax_keys_on_rows(_dot(kw_ref[pl.ds(ws, nw), :], qt_cat) + heads(bias_w)).astype(BF)
    wblk = ws // wbk
    o_w = _dot(vwt_ref[wblk], p_w[0:wbk])
    for jb in range(1, nw // wbk):
        o_w = o_w + _dot(vwt_ref[wblk + jb], p_w[jb * wbk:(jb + 1) * wbk])

    cj = lax.broadcasted_iota(jnp.int32, (LANES, ncp), 0) * SEL_BLOCK
    cn = lax.broadcasted_iota(jnp.int32, (LANES, ncp), 1) * CMP_STRIDE
    cover_t = jnp.where((cn <= cj + SEL_BLOCK - 1) & (cn + CMP_LEN - 1 >= cj), 1.0, 0.0).astype(BF)
    hi, mid, lo = _split3(psum)
    imp_t = _dot(cover_t, hi) + _dot(cover_t, mid) + _dot(cover_t, lo)
    blk = lax.broadcasted_iota(jnp.int32, (LANES, tq), 0)
    sel_t = _top_n_mask(_sel_scores(imp_t, blk, tpos((LANES, tq)), ns), blk, 0, min(SEL_TOPN, ns))
    bsel_scr[...] = jnp.where(sel_t > 0.5, 0.0, NEG)

    def sel_chunk(c, carry, causal):
        m, acc = carry
        s = _dot(ks_ref[pl.ds(pl.multiple_of(c * ck, ck), ck), :], qt_cat)
        brows = heads(bsel_scr[pl.ds(pl.multiple_of(c * bpk, bpk), bpk), :])
        s = jnp.concatenate([s[j * SEL_BLOCK:(j + 1) * SEL_BLOCK] + brows[j:j + 1] for j in range(bpk)], axis=0)
        if causal:
            kpos = lax.broadcasted_iota(jnp.int32, (ck, tq), 0) + c * ck
            s = s + heads(jnp.where(kpos <= tpos((ck, tq)), 0.0, NEG))
        m_new = jnp.maximum(m, jnp.max(s, axis=0, keepdims=True))
        p = jnp.exp(s - m_new).astype(BF)
        acc = jnp.exp(m - m_new) * acc
        for jb in range(ck // vb):
            acc = acc + _dot(vst_ref[c * (ck // vb) + jb], p[jb * vb:(jb + 1) * vb])
        return m_new, acc

    init = (jnp.full((1, rep * tq), NEG, F32), jnp.zeros((vst_ref.shape[1], rep * tq), F32))
    fin = lax.fori_loop(0, n_chunks - 1, functools.partial(sel_chunk, causal=False), init)
    _, acc_s = sel_chunk(n_chunks - 1, fin, True)
    o_s = acc_s[:dh] / jnp.maximum(acc_s[dh:dh + 1], 1e-30)

    gt = _sigmoid(gt_ref[...])

    def gate(br):
        return jnp.concatenate([gt[3 * r + br:3 * r + br + 1] for r in range(rep)], axis=1)

    o_all = gate(0) * o_c + gate(1) * o_s + gate(2) * o_w
    o_ref[...] = jnp.concatenate([o_all[:, r * tq:(r + 1) * tq] for r in range(rep)], axis=0).T


def nsa_attn_prompt(q, gates_t, kc, vct, ks, vst, kw, vwt, *, tq, ck):
    bsz, t_len, _ = q.shape
    ncp = kc.shape[2]
    gw = NSA_REP * NSA_DH
    ns = -(-t_len // SEL_BLOCK)
    vb, wbk = vst.shape[-1], vwt.shape[-1]
    assert tq == LANES and ck % vb == 0 and tq % wbk == 0 and (ck // SEL_BLOCK) % 8 == 0 and ck % tq == 0
    assert ns <= LANES and t_len % ck == 0 and t_len >= WINDOW + tq
    ones_rows = 16
    ones = jnp.concatenate([jnp.ones(vst.shape[:3] + (1, vb), BF),
                            jnp.zeros(vst.shape[:3] + (ones_rows - 1, vb), BF)], axis=3)
    vst = jnp.concatenate([vst, ones], axis=3)

    def blk_spec(x):
        return pl.BlockSpec((None, None) + x.shape[2:], lambda b, g, i: (b, g) + (0,) * (x.ndim - 2))

    return pl.pallas_call(
        functools.partial(_nsa_prompt_kernel, tq=tq, ncp=ncp, ns=ns, ck=ck, nw=WINDOW + tq),
        scratch_shapes=[pltpu.VMEM((LANES, tq), F32)],
        grid=(bsz, NSA_GROUPS, t_len // tq),
        in_specs=[pl.BlockSpec((None, tq, gw), lambda b, g, i: (b, i, g)),
                  pl.BlockSpec((None, None, NSA_REP * 3, tq), lambda b, g, i: (b, g, 0, i)),
                  blk_spec(kc), blk_spec(vct), blk_spec(ks), blk_spec(vst), blk_spec(kw), blk_spec(vwt)],
        out_specs=pl.BlockSpec((None, tq, gw), lambda b, g, i: (b, i, g)),
        out_shape=jax.ShapeDtypeStruct((bsz, t_len, NSA_GROUPS * gw), F32),
        compiler_params=_cparams("parallel", "parallel", "arbitrary"),
        name="nsa_attn_prompt",
    )(q, gates_t, kc, vct, ks, vst, kw, vwt)


def _nsa_sample_kernel(pt_ref, q_ref, gt_ref, kvn_ref, wn_ref, *rest, n_pages, tq, past_len, wb):
    del pt_ref
    cmp_refs, sel_refs = rest[:n_pages], rest[n_pages:2 * n_pages]
    win_ref, pe_ref, w1_ref, w2_ref, o_ref = rest[2 * n_pages:]
    rep, dh, ng = NSA_REP, NSA_DH, NSA_GROUPS
    rows = rep * tq
    seg = PAGE_SIZE
    ppp = PAGE_SIZE // CMP_STRIDE
    ncp = n_pages * ppp
    nk = past_len + seg
    ns = -(-(past_len + tq) // SEL_BLOCK)

    def tpos(shape):
        return (lax.broadcasted_iota(jnp.int32, shape, 0) & (tq - 1)) + past_len

    kcvc = []
    for c in range(2):
        pieces = jnp.concatenate([cmp_refs[p][c, g].astype(F32) for g in range(ng) for p in range(n_pages)], axis=0)
        a, b = _compress_halves(pieces, pe_ref[c], w1_ref[c])
        per_group = []
        for g in range(ng):
            h = _silu(a[g * ncp:(g + 1) * ncp] + pltpu.roll(b[g * ncp:(g + 1) * ncp], ncp - 1, 0))
            per_group.append(_dot(h.astype(BF), w2_ref[c]).astype(BF))
        kcvc.append(per_group)

    q = q_ref[...] * (dh ** -0.5)
    kvn = kvn_ref[...]
    wn = wn_ref[...]
    kvw = ng * dh

    def q4_of(g):
        return jnp.concatenate([q[:, (g * rep + r) * dh:(g * rep + r + 1) * dh] for r in range(rep)],
                               axis=0).astype(BF)

    def new_seg(x, col):
        return jnp.concatenate([x[:, col:col + dh], jnp.zeros((seg - tq, dh), F32)], axis=0).astype(BF)

    n_idx = lax.broadcasted_iota(jnp.int32, (rows, ncp), 1)
    bias_c = jnp.where(CMP_STRIDE * n_idx + CMP_LEN - 1 <= tpos((rows, ncp)), 0.0, NEG)
    q4s, o_cs, psums = [], [], []
    for g in range(ng):
        q4 = q4_of(g)
        p_c = _bias_softmax(_dot_nt(q4, kcvc[0][g]), bias_c)
        q4s.append(q4)
        o_cs.append(_dot(p_c.astype(BF), kcvc[1][g]))
        psums.append(_sum_heads(p_c, tq))
    cj = lax.broadcasted_iota(jnp.int32, (LANES, ncp), 0) * SEL_BLOCK
    cn = lax.broadcasted_iota(jnp.int32, (LANES, ncp), 1) * CMP_STRIDE
    cover_t = jnp.where((cn <= cj + SEL_BLOCK - 1) & (cn + CMP_LEN - 1 >= cj), 1.0, 0.0).astype(BF)
    hi, mid, lo = _split3(jnp.concatenate(psums, axis=0))
    imp_t = _dot_nt(cover_t, hi) + _dot_nt(cover_t, mid) + _dot_nt(cover_t, lo)
    blk = lax.broadcasted_iota(jnp.int32, (LANES, ng * tq), 0)
    tp_t = (lax.broadcasted_iota(jnp.int32, (LANES, ng * tq), 1) & (tq - 1)) + past_len
    sel_t = _top_n_mask(_sel_scores(imp_t, blk, tp_t, ns), blk, 0, min(SEL_TOPN, ns))
    ej = lax.broadcasted_iota(jnp.int32, (LANES, nk), 0)
    ek = lax.broadcasted_iota(jnp.int32, (LANES, nk), 1)
    expand = jnp.where(ek // SEL_BLOCK == ej, 1.0, 0.0).astype(BF)
    sel_cols = jnp.concatenate([sel_t[:, g * tq:(g + 1) * tq] for g in range(ng) for _ in range(rep)], axis=1)
    mk_all = _dot_tn(sel_cols.astype(BF), expand)
    kpos = lax.broadcasted_iota(jnp.int32, (rows, nk), 1)
    causal = kpos <= tpos((rows, nk))
    wp = lax.broadcasted_iota(jnp.int32, (rows, wb + seg), 1) + (past_len - wb)
    tw = tpos((rows, wb + seg))
    bias_w = jnp.where((wp <= tw) & (wp > tw - WINDOW), 0.0, NEG)

    gt = _sigmoid(gt_ref[...])
    for g in range(ng):
        q4 = q4s[g]
        sc = jnp.concatenate([_dot(q4, sel_refs[p][0, g]) for p in range(n_pages)]
                             + [_dot_nt(q4, new_seg(kvn, 2 * kvw + g * dh))], axis=1)
        bias = jnp.where(causal & (mk_all[g * rows:(g + 1) * rows] > 0.5), 0.0, NEG)
        p_s = _bias_softmax(sc, bias).astype(BF)
        o_s = _dot(p_s[:, past_len:], new_seg(kvn, 3 * kvw + g * dh))
        for p in range(n_pages):
            o_s = o_s + _dot_nt(p_s[:, p * seg:(p + 1) * seg], sel_refs[p][1, g])
        sw = jnp.concatenate([_dot(q4, win_ref[0, g]), _dot_nt(q4, new_seg(wn, g * dh))], axis=1)
        p_w = _bias_softmax(sw, bias_w).astype(BF)
        o_w = _dot_nt(p_w[:, :wb], win_ref[1, g]) + _dot(p_w[:, wb:], new_seg(wn, kvw + g * dh))
        c0 = g * rep * 3
        o4 = _gate_cols(gt, c0, 0) * o_cs[g] + _gate_cols(gt, c0, 1) * o_s + _gate_cols(gt, c0, 2) * o_w
        o_ref[:, g * rep * dh:(g + 1) * rep * dh] = jnp.concatenate(
            [o4[r * tq:(r + 1) * tq] for r in range(rep)], axis=1)


def nsa_attn_sample(q, gates, kv_new, win_new, cmp_pages, sel_pages, win_t, page_table, pe_ab, w1_cat, w2):
    bsz, tq, hq = q.shape
    n_pages = page_table.shape[1]
    wb = win_t.shape[-1]
    assert tq & (tq - 1) == 0 and tq <= PAGE_SIZE

    def row_spec(w):
        return pl.BlockSpec((None, tq, w), lambda b, pt: (b, 0, 0))

    def page_spec(shape, p):
        return pl.BlockSpec((None,) + shape, lambda b, pt: (pt[b, p],) + (0,) * len(shape))

    def full_spec(x):
        return pl.BlockSpec(x.shape, lambda b, pt: (0,) * x.ndim)

    in_specs = ([row_spec(hq), row_spec(gates.shape[2]), row_spec(kv_new.shape[2]), row_spec(win_new.shape[2])]
                + [page_spec(cmp_pages.shape[1:], p) for p in range(n_pages)]
                + [page_spec(sel_pages.shape[1:], p) for p in range(n_pages)]
                + [pl.BlockSpec((None,) + win_t.shape[1:], lambda b, pt: (b, 0, 0, 0, 0)),
                   full_spec(pe_ab), full_spec(w1_cat), full_spec(w2)])
    return pl.pallas_call(
        functools.partial(_nsa_sample_kernel, n_pages=n_pages, tq=tq, past_len=n_pages * PAGE_SIZE, wb=wb),
        grid_spec=pltpu.PrefetchScalarGridSpec(
            num_scalar_prefetch=1, grid=(bsz,), in_specs=in_specs, out_specs=row_spec(hq)),
        out_shape=jax.ShapeDtypeStruct((bsz, tq, hq), F32),
        compiler_params=_cparams("parallel"),
        name="nsa_attn_sample",
    )(page_table, q, gates, kv_new, win_new, *([cmp_pages] * n_pages), *([sel_pages] * n_pages),
      win_t, pe_ab, w1_cat, w2)


def _pad_cols(w, n):
    return jnp.pad(w, ((0, 0), (0, n - w.shape[1])))


def _group_major(x, bsz, t_len):
    return x.reshape(bsz, t_len, NSA_GROUPS, NSA_DH).transpose(0, 2, 1, 3)


def _group_major_t(x, bsz, t_len, blk):
    x5 = x.reshape(bsz, t_len // blk, blk, NSA_GROUPS, NSA_DH)
    return x5.transpose(0, 3, 1, 4, 2)


def _ffn_chunk(d_ff):
    n = d_ff // LANES
    for parts in range(2, n + 1):
        if n % parts == 0:
            return (n // parts) * LANES
    return d_ff


def _to_time_major(x, bsz, t_len):
    return x.reshape(bsz, t_len, -1).transpose(1, 0, 2).reshape(t_len * bsz, -1)


def _to_batch_major(x, bsz, t_len):
    return x.reshape(t_len, bsz, -1).transpose(1, 0, 2).reshape(bsz * t_len, -1)


def kernel(x_prompt, x_sample, cache_gla_state, cache_conv, cache_nsa_kv, cache_nsa_win, cache_mem_kv,
           cache_ffn_conv, page_table, mem_prompt, norm_mix, norm_mem, norm_x, norm_ffn, norm_final,
           w_in_a, w_gate_a, b_gate_a, g_gla_out, w_dw_b, b_dw_b, g_ln_b, b_ln_b, w_out_a,
           w_in_c, pe_cmp, w_cmp1, w_cmp2, w_out_c, w_xq, w_mem_kv, w_xo, w_up, w_ffn_dw, b_ffn_dw, w_down):
    bp, tp, d = x_prompt.shape
    bs, ts, _ = x_sample.shape
    depth = norm_mix.shape[0]
    n_mem = mem_prompt.shape[1]
    d_ff = w_down.shape[1]
    hk, hv = GLA_HEADS * GLA_DK, GLA_HEADS * GLA_DV
    kvw = NSA_GROUPS * NSA_DH
    hq = NSA_HEADS * NSA_DH

    xp = x_prompt.reshape(bp * tp, d)
    xs = x_sample.reshape(bs * ts, d)
    tm_p = 512
    tm_s = bs * ts

    outs = {k: [] for k in ("gla_p", "gla_s", "conv_p", "conv_s", "nsa_p", "nsa_s", "win_p", "win_s",
                            "mem_p", "ffn_p", "ffn_s")}
    for l in range(depth):
        i = l // 2
        if l % 2 == 0:
            wi = w_in_a[i]
            c0, c1, c2, c3, c4 = hk, 2 * hk, 2 * hk + hv, 2 * hk + 2 * hv, 2 * hk + 2 * hv + GLA_RANK
            w_a = jnp.concatenate([wi[:, c4:], wi[:, c1:c2], wi[:, c2:c3], wi[:, :c0], wi[:, c0:c1],
                                   _pad_cols(wi[:, c3:c4], LANES)], axis=1).astype(BF)
            splits = (2 * CONV_CH, hv, hv, hk, hk, LANES)
            wg = jnp.pad(w_gate_a[i], ((0, LANES - GLA_RANK), (0, 0))).astype(BF)
            bg = b_gate_a[i].reshape(1, hk)
            go = g_gla_out[i].reshape(1, hv)
            w_out = w_out_a[i].astype(BF)
            vecs = [v.reshape(1, CONV_CH) for v in (b_dw_b[i], g_ln_b[i], b_ln_b[i])]

            def mixer(x, bsz, t_len, tm, s0, hist_tm, n_outer, ctm, step):
                u, v, r, q, k, a = norm_matmul(x, norm_mix[l], w_a, splits, tm)
                sh = lambda z: z.reshape(bsz, t_len, -1)
                og, s_new = gla(sh(q), sh(k), sh(v), sh(r), sh(a), wg, bg, go, s0, 1024)
                if step > 1:
                    u = _to_time_major(u, bsz, t_len)
                c, hist_new = convmod(u, hist_tm, w_dw_b[i], *vecs, n_outer, ctm, step)
                if step > 1:
                    c = _to_batch_major(c, bsz, t_len)
                y = matmul_res([og.reshape(bsz * t_len, hv), c], [w_out[:hv], w_out[hv:]], x, tm)
                return y, s_new, hist_new

            xp, sp, cp = mixer(xp, bp, tp, tm_p, jnp.zeros((bp, GLA_HEADS, GLA_DK, GLA_DV), F32),
                               jnp.zeros((bp, CONV_WIDTH - 1, CONV_CH), F32), bp, 512, 1)
            hist_s = cache_conv[i].transpose(1, 0, 2).reshape(1, (CONV_WIDTH - 1) * bs, CONV_CH)
            xs, ss, cs = mixer(xs, bs, ts, tm_s, cache_gla_state[i], hist_s, 1, bs * ts, bs)
            cs = cs.reshape(CONV_WIDTH - 1, bs, CONV_CH).transpose(1, 0, 2)
            outs["gla_p"].append(sp)
            outs["gla_s"].append(ss)
            outs["conv_p"].append(cp)
            outs["conv_s"].append(cs)
        else:
            n_in = hq + 6 * kvw + 3 * NSA_HEADS
            n_pad = -(-n_in // LANES) * LANES
            w_c = _pad_cols(w_in_c[i], n_pad).astype(BF)
            splits = (hq, 4 * kvw, 2 * kvw, n_pad - hq - 6 * kvw)
            w_out = w_out_c[i].astype(BF)
            pw = CMP_STRIDE * NSA_DH
            pe_ab = pe_cmp[i].reshape(2, 2, pw)
            w1_ab = w_cmp1[i].reshape(2, 2, pw, CMP_HIDDEN)
            w1_cat = jnp.concatenate([w1_ab[:, 0], w1_ab[:, 1]], axis=-1).astype(BF)
            w2 = w_cmp2[i].astype(BF)

            def pieces_of(rows, bsz, n):
                r6 = rows[:, :n * CMP_STRIDE].reshape(bsz, n, CMP_STRIDE, 2, NSA_GROUPS, NSA_DH)
                return r6.transpose(0, 3, 4, 1, 2, 5).reshape(bsz, 2, NSA_GROUPS, n, pw).astype(BF)

            q, kv4, win, gl = norm_matmul(xp, norm_mix[l], w_c, splits, tm_p)
            kv4b = kv4.reshape(bp, tp, 4 * kvw)
            kcvc = compress(pieces_of(kv4b[:, :, :2 * kvw], bp, tp // CMP_STRIDE), pe_ab, w1_cat, w2)
            ks = _group_major(kv4[:, 2 * kvw:3 * kvw], bp, tp).astype(BF)
            vst = _group_major_t(kv4[:, 3 * kvw:], bp, tp, 2 * LANES).astype(BF)
            kw = _group_major(win[:, :kvw], bp, tp).astype(BF)
            vwt = _group_major_t(win[:, kvw:], bp, tp, LANES).astype(BF)
            gates_t = gl[:, :3 * NSA_HEADS].reshape(bp, tp, NSA_GROUPS, NSA_REP * 3).transpose(0, 2, 3, 1)
            o = nsa_attn_prompt(q.reshape(bp, tp, hq), gates_t, kcvc[:, 0].astype(BF),
                                kcvc[:, 1].transpose(0, 1, 3, 2).astype(BF), ks, vst, kw, vwt,
                                tq=Q_BLOCK, ck=1024)
            xp = matmul_res([o.reshape(bp * tp, hq)], [w_out], xp, tm_p)
            win_rows = min(WINDOW, tp)
            outs["nsa_p"].append(kv4.reshape(bp, tp, 4, NSA_GROUPS, NSA_DH))
            outs["win_p"].append(win.reshape(bp, tp, 2, NSA_GROUPS, NSA_DH)[:, tp - win_rows:])

            q, kv4, win, gl = norm_matmul(xs, norm_mix[l], w_c, splits, tm_s)
            pool = cache_nsa_kv[i]
            n_phys = pool.shape[0]
            ppp = PAGE_SIZE // CMP_STRIDE
            cmp_pages = (pool[:, :, :2].reshape(n_phys, ppp, CMP_STRIDE, 2, NSA_GROUPS, NSA_DH)
                         .transpose(0, 3, 4, 1, 2, 5).reshape(n_phys, 2, NSA_GROUPS, ppp, pw).astype(BF))
            sel_pages = pool[:, :, 2:].transpose(0, 2, 3, 4, 1).astype(BF)
            win_cache = cache_nsa_win[i]
            wb = win_cache.shape[1]
            win_t = win_cache.transpose(0, 2, 3, 4, 1).astype(BF)
            o = nsa_attn_sample(q.reshape(bs, ts, hq), gl.reshape(bs, ts, -1), kv4.reshape(bs, ts, 4 * kvw),
                                win.reshape(bs, ts, 2 * kvw), cmp_pages, sel_pages, win_t, page_table,
                                pe_ab, w1_cat, w2)
            xs = matmul_res([o.reshape(bs * ts, hq)], [w_out], xs, tm_s)
            outs["nsa_s"].append(kv4.reshape(bs, ts, 4, NSA_GROUPS, NSA_DH))
            win_new = win.reshape(bs, ts, 2, NSA_GROUPS, NSA_DH)
            outs["win_s"].append(jnp.concatenate([win_cache, win_new], axis=1)[:, ts:])

        xhw = X_HEADS * X_DH
        (mkv,) = norm_matmul(mem_prompt.reshape(bp * n_mem, d), norm_mem[l], w_mem_kv[l].astype(BF),
                             (2 * xhw,), bp * n_mem)
        outs["mem_p"].append(mkv.reshape(bp, n_mem, 2, X_HEADS, X_DH))
        w_q = w_xq[l].astype(BF)
        w_o = w_xo[l].astype(BF)
        (qx,) = norm_matmul(xp, norm_x[l], w_q, (xhw,), tm_p)
        ox = xattn(qx.reshape(bp, tp, xhw), mkv.reshape(1, bp, n_mem, 2, X_HEADS, X_DH), 0, 512)
        xp = matmul_res([ox.reshape(bp * tp, xhw)], [w_o], xp, tm_p)
        (qx,) = norm_matmul(xs, norm_x[l], w_q, (xhw,), tm_s)
        ox = xattn(qx.reshape(bs, ts, xhw), cache_mem_kv, l, ts)
        xs = matmul_res([ox.reshape(bs * ts, xhw)], [w_o], xs, tm_s)

        wu = w_up[l].astype(BF)
        wd = w_down[l].astype(BF)
        fg = norm_final if l == depth - 1 else None
        cw = _ffn_chunk(d_ff)
        xp, hfp = conv_ffn(xp, norm_ffn[l], jnp.zeros((bp, FFN_WIDTH - 1, 2 * d_ff), F32), wu, w_ffn_dw[l],
                           b_ffn_dw[l], wd, bp, 512, 1, cw, fg)
        hist_s = cache_ffn_conv[l].transpose(1, 0, 2).reshape(1, (FFN_WIDTH - 1) * bs, 2 * d_ff)
        xs_tm, hfs = conv_ffn(_to_time_major(xs, bs, ts), norm_ffn[l], hist_s, wu, w_ffn_dw[l], b_ffn_dw[l], wd,
                              1, bs * ts, bs, cw, fg)
        xs = _to_batch_major(xs_tm, bs, ts)
        outs["ffn_p"].append(hfp)
        outs["ffn_s"].append(hfs.reshape(FFN_WIDTH - 1, bs, 2 * d_ff).transpose(1, 0, 2))

    st = jnp.stack
    return (xp.reshape(bp, tp, d), xs.reshape(bs, ts, d),
            st(outs["gla_p"]), st(outs["gla_s"]), st(outs["conv_p"]), st(outs["conv_s"]),
            st(outs["nsa_p"]), st(outs["nsa_s"]), st(outs["win_p"]), st(outs["win_s"]),
            st(outs["mem_p"]), st(outs["ffn_p"]), st(outs["ffn_s"]))
```

```python
import functools
import math

import jax
import jax.numpy as jnp
from jax import lax
from jax.experimental import pallas as pl
from jax.experimental.pallas import tpu as pltpu

EPS = 1e-6
NEG = -1e30
F32 = jnp.float32
BF = jnp.bfloat16

V7X_VMEM_BYTES = 64 * 1024 * 1024
VMEM_LIMIT = V7X_VMEM_BYTES - 8 * 1024 * 1024
LANES = 128

GLA_HEADS, GLA_DK, GLA_DV, GLA_RANK, GLA_TAU, GLA_CHUNK = 4, 64, 128, 16, 16.0, 16
GLA_UNROLL = 4
CONV_CH, CONV_WIDTH = 512, 31
NSA_HEADS, NSA_GROUPS, NSA_DH = 16, 4, 64
NSA_REP = NSA_HEADS // NSA_GROUPS
CMP_LEN, CMP_STRIDE, CMP_HIDDEN = 32, 16, 64
SEL_BLOCK, SEL_TOPN, WINDOW, Q_BLOCK = 64, 16, 512, 128
X_HEADS, X_DH = 4, 128
FFN_WIDTH = 3
PAGE_SIZE = 128


def _cparams(*sem):
    return pltpu.CompilerParams(dimension_semantics=sem, vmem_limit_bytes=VMEM_LIMIT)


def _rms(x, g):
    return x * lax.rsqrt(jnp.mean(x * x, axis=-1, keepdims=True) + EPS) * g


def _sigmoid(x):
    return 1.0 / (1.0 + jnp.exp(-x))


def _silu(x):
    return x * _sigmoid(x)


def _dot(a, b):
    return jnp.dot(a, b, preferred_element_type=F32)


def _dot_nt(a, b):
    return lax.dot_general(a, b, (((1,), (1,)), ((), ())), preferred_element_type=F32)


def _dot_tn(a, b):
    return lax.dot_general(a, b, (((0,), (0,)), ((), ())), preferred_element_type=F32)


def _split3(x):
    hi = x.astype(BF)
    r1 = x - hi.astype(F32)
    mid = r1.astype(BF)
    lo = (r1 - mid.astype(F32)).astype(BF)
    return hi, mid, lo


def _norm_matmul_kernel(x_ref, g_ref, w_ref, *o_refs, splits):
    xn = _rms(x_ref[...], g_ref[...]).astype(BF)
    off = 0
    for o_ref, n in zip(o_refs, splits):
        o_ref[...] = _dot(xn, w_ref[:, off:off + n])
        off += n


def norm_matmul(x, g, w, splits, tm):
    m, d = x.shape
    n = w.shape[1]
    assert sum(splits) == n and m % tm == 0
    outs = pl.pallas_call(
        functools.partial(_norm_matmul_kernel, splits=tuple(splits)),
        grid=(m // tm,),
        in_specs=[pl.BlockSpec((tm, d), lambda i: (i, 0)),
                  pl.BlockSpec((1, d), lambda i: (0, 0)),
                  pl.BlockSpec((d, n), lambda i: (0, 0))],
        out_specs=[pl.BlockSpec((tm, s), lambda i: (i, 0)) for s in splits],
        out_shape=[jax.ShapeDtypeStruct((m, s), F32) for s in splits],
        compiler_params=_cparams("parallel"),
        name="norm_matmul",
    )(x, g.reshape(1, d), w)
    return outs


def _matmul_res_kernel(*refs, n_in):
    a_refs, w_refs = refs[:n_in], refs[n_in:2 * n_in]
    res_ref, o_ref = refs[2 * n_in], refs[2 * n_in + 1]
    acc = _dot(a_refs[0][...].astype(BF), w_refs[0][...])
    for a_ref, w_ref in zip(a_refs[1:], w_refs[1:]):
        acc = acc + _dot(a_ref[...].astype(BF), w_ref[...])
    o_ref[...] = res_ref[...] + acc


def matmul_res(a_list, w_list, res, tm):
    m, d = res.shape
    n_in = len(a_list)
    return pl.pallas_call(
        functools.partial(_matmul_res_kernel, n_in=n_in),
        grid=(m // tm,),
        in_specs=([pl.BlockSpec((tm, a.shape[1]), lambda i: (i, 0)) for a in a_list]
                  + [pl.BlockSpec(w.shape, lambda i: (0, 0)) for w in w_list]
                  + [pl.BlockSpec((tm, d), lambda i: (i, 0))]),
        out_specs=pl.BlockSpec((tm, d), lambda i: (i, 0)),
        out_shape=jax.ShapeDtypeStruct((m, d), F32),
        compiler_params=_cparams("parallel"),
        name="matmul_res",
    )(*a_list, *w_list, res)


def _gla_kernel(q_ref, k_ref, v_ref, r_ref, a_ref, wg_ref, bg_ref, go_ref, s0_ref,
                o_ref, sout_ref, s_scr, b_scr, *, chunk, n_chunks):
    j = pl.program_id(1)
    tt = chunk * n_chunks
    blk = min(LANES, tt)

    @pl.when(j == 0)
    def _():
        s_scr[...] = s0_ref[...]

    ri = lax.broadcasted_iota(jnp.int32, (chunk, chunk), 0)
    ci = lax.broadcasted_iota(jnp.int32, (chunk, chunk), 1)
    tri = (ri >= ci).astype(F32)
    eye = (lax.broadcasted_iota(jnp.int32, (GLA_DK, GLA_DK), 0)
           == lax.broadcasted_iota(jnp.int32, (GLA_DK, GLA_DK), 1)).astype(F32)

    rb = lax.broadcasted_iota(jnp.int32, (blk, blk), 0)
    cb = lax.broadcasted_iota(jnp.int32, (blk, blk), 1)
    tri_blk = jnp.where((rb >= cb) & (rb // chunk == cb // chunk), 1.0, 0.0).astype(BF)
    for nb in range(tt // blk):
        rs = slice(nb * blk, (nb + 1) * blk)
        z = _dot(a_ref[rs, :].astype(BF), wg_ref[...]) + bg_ref[...]
        log_a = (jnp.minimum(z, 0.0) - jnp.log(1.0 + jnp.exp(-jnp.abs(z)))) / GLA_TAU
        hi, mid, lo = _split3(log_a)
        b_scr[rs, :] = _dot(tri_blk, hi) + _dot(tri_blk, mid) + _dot(tri_blk, lo)

    def body(i, carry):
        sl = pl.ds(pl.multiple_of(i * chunk, chunk), chunk)
        qc = q_ref[sl, :] * (GLA_DK ** -0.5)
        kc = k_ref[sl, :]
        b = b_scr[sl, :]
        new_s = []
        for h in range(GLA_HEADS):
            ks = slice(h * GLA_DK, (h + 1) * GLA_DK)
            vs = slice(h * GLA_DV, (h + 1) * GLA_DV)
            bh = b[:, ks]
            b_last = bh[chunk - 1:chunk, :]
            qe = (qc[:, ks] * jnp.exp(bh)).astype(BF)
            ke = (kc[:, ks] * jnp.exp(-bh)).astype(BF)
            kl = (kc[:, ks] * jnp.exp(b_last - bh)).astype(BF)
            vh = v_ref[sl, vs].astype(BF)
            s = carry[h]
            att = _dot_nt(qe, ke) * tri
            o = _dot(att.astype(BF), vh) + _dot(qe, s.astype(BF))
            decay = jnp.exp(jnp.sum(eye * b_last, axis=1, keepdims=True))
            new_s.append(decay * s + _dot_tn(kl, vh))
            o = o * lax.rsqrt(jnp.mean(o * o, axis=-1, keepdims=True) + EPS) * go_ref[:, vs]
            o_ref[sl, vs] = o * _silu(r_ref[sl, vs])
        return tuple(new_s)

    s_fin = lax.fori_loop(0, n_chunks, body, tuple(s_scr[h] for h in range(GLA_HEADS)),
                          unroll=math.gcd(n_chunks, GLA_UNROLL))
    for h in range(GLA_HEADS):
        s_scr[h] = s_fin[h]
    sout_ref[...] = s_scr[...]


def gla(q, k, v, r, a, wg, bg, go, s0, tt):
    bsz, t_len, _ = q.shape
    chunk = math.gcd(t_len, GLA_CHUNK)
    tt = min(tt, t_len)
    hk, hv = GLA_HEADS * GLA_DK, GLA_HEADS * GLA_DV

    def tspec(w):
        return pl.BlockSpec((None, tt, w), lambda b, j: (b, j, 0))

    sspec = pl.BlockSpec((None, GLA_HEADS, GLA_DK, GLA_DV), lambda b, j: (b, 0, 0, 0))
    return pl.pallas_call(
        functools.partial(_gla_kernel, chunk=chunk, n_chunks=tt // chunk),
        grid=(bsz, t_len // tt),
        in_specs=[tspec(hk), tspec(hk), tspec(hv), tspec(hv), tspec(LANES),
                  pl.BlockSpec((LANES, hk), lambda b, j: (0, 0)),
                  pl.BlockSpec((1, hk), lambda b, j: (0, 0)),
                  pl.BlockSpec((1, hv), lambda b, j: (0, 0)),
                  sspec],
        out_specs=[tspec(hv), sspec],
        out_shape=[jax.ShapeDtypeStruct((bsz, t_len, hv), F32),
                   jax.ShapeDtypeStruct((bsz, GLA_HEADS, GLA_DK, GLA_DV), F32)],
        scratch_shapes=[pltpu.VMEM((GLA_HEADS, GLA_DK, GLA_DV), F32), pltpu.VMEM((tt, hk), F32)],
        compiler_params=_cparams("parallel", "arbitrary"),
        name="gla",
    )(q, k, v, r, a, wg, bg, go, s0)


def _convmod_kernel(u1_ref, u2_ref, hist_ref, w_ref, b_ref, g_ref, bl_ref, c_ref, hout_ref, gpad,
                    *, tm, step, off):
    hrows = (CONV_WIDTH - 1) * step
    base = off - hrows

    @pl.when(pl.program_id(1) == 0)
    def _():
        gpad[base:off, :] = hist_ref[...]

    gpad[off:off + tm, :] = u1_ref[...] * _sigmoid(u2_ref[...])
    acc = b_ref[...] + w_ref[0:1, :] * gpad[base:base + tm, :]
    for kk in range(1, CONV_WIDTH):
        acc = acc + w_ref[kk:kk + 1, :] * gpad[base + kk * step:base + kk * step + tm, :]
    d = acc - jnp.mean(acc, axis=-1, keepdims=True)
    c = d * lax.rsqrt(jnp.mean(d * d, axis=-1, keepdims=True) + EPS) * g_ref[...] + bl_ref[...]
    c_ref[...] = _silu(c)
    new_hist = gpad[base + tm:off + tm, :]
    hout_ref[...] = new_hist
    gpad[base:off, :] = new_hist


def convmod(u, hist, w, b, g, bl, n_outer, tm, step):
    m = u.shape[0]
    rows = m // n_outer
    nt = rows // tm
    hrows = (CONV_WIDTH - 1) * step
    assert tm >= hrows or nt == 1
    off = -(-hrows // 8) * 8
    ch = CONV_CH
    vec = pl.BlockSpec((1, ch), lambda o, t: (0, 0))
    return pl.pallas_call(
        functools.partial(_convmod_kernel, tm=tm, step=step, off=off),
        grid=(n_outer, nt),
        in_specs=[pl.BlockSpec((tm, ch), lambda o, t: (o * nt + t, 0)),
                  pl.BlockSpec((tm, ch), lambda o, t: (o * nt + t, 1)),
                  pl.BlockSpec((None, hrows, ch), lambda o, t: (o, 0, 0)),
                  pl.BlockSpec((CONV_WIDTH, ch), lambda o, t: (0, 0)),
                  vec, vec, vec],
        out_specs=[pl.BlockSpec((tm, ch), lambda o, t: (o * nt + t, 0)),
                   pl.BlockSpec((None, hrows, ch), lambda o, t: (o, 0, 0))],
        out_shape=[jax.ShapeDtypeStruct((m, ch), F32),
                   jax.ShapeDtypeStruct((n_outer, hrows, ch), F32)],
        scratch_shapes=[pltpu.VMEM((off + tm, ch), F32)],
        compiler_params=_cparams("parallel", "arbitrary"),
        name="convmod",
    )(u, u, hist, w, b, g, bl)


def _xattn_kernel(q_ref, kv_ref, o_ref):
    q = q_ref[...] * (X_DH ** -0.5)
    for h in range(X_HEADS):
        sl = slice(h * X_DH, (h + 1) * X_DH)
        kh = kv_ref[:, 0, h, :].astype(BF)
        vh = kv_ref[:, 1, h, :].astype(BF)
        s = _dot_nt(q[:, sl].astype(BF), kh)
        p = jnp.exp(s - jnp.max(s, axis=-1, keepdims=True))
        p = p / jnp.sum(p, axis=-1, keepdims=True)
        o_ref[:, sl] = _dot(p.astype(BF), vh)


def xattn(q, kv, layer, tq):
    bsz, t_len, hw = q.shape
    n_mem = kv.shape[2]
    return pl.pallas_call(
        _xattn_kernel,
        grid=(bsz, t_len // tq),
        in_specs=[pl.BlockSpec((None, tq, hw), lambda b, i: (b, i, 0)),
                  pl.BlockSpec((None, None, n_mem, 2, X_HEADS, X_DH), lambda b, i: (layer, b, 0, 0, 0, 0))],
        out_specs=pl.BlockSpec((None, tq, hw), lambda b, i: (b, i, 0)),
        out_shape=jax.ShapeDtypeStruct((bsz, t_len, hw), F32),
        compiler_params=_cparams("parallel", "parallel"),
        name="xattn",
    )(q, kv)


def _ffn_kernel(*refs, tm, step, final_norm):
    (x_ref, g_ref, ha_ref, hb_ref, wua_ref, wub_ref, wda_ref, wdb_ref, ba_ref, bb_ref, wdn_ref) = refs[:11]
    rest = refs[11:]
    if final_norm:
        gf_ref, rest = rest[0], rest[1:]
    o_ref, sa_ref, sb_ref, xn_scr, acc_scr, halo_a, halo_b = rest
    t = pl.program_id(1)
    j = pl.program_id(2)
    hrows = (FFN_WIDTH - 1) * step

    @pl.when(j == 0)
    def _():
        xn_scr[...] = _rms(x_ref[...], g_ref[...]).astype(BF)

    @pl.when(t == 0)
    def _():
        halo_a[j] = ha_ref[...]
        halo_b[j] = hb_ref[...]

    def conv_half(wu_ref, wd_ref, b_ref, halo, s_ref):
        u = _dot(xn_scr[...], wu_ref[...])
        h = halo[j]
        if step % 8 == 0:
            u1 = jnp.concatenate([h[step:], u[:tm - step]], axis=0)
            u2 = jnp.concatenate([h, u[:tm - 2 * step]], axis=0)
        else:
            assert step == 1
            row = lax.broadcasted_iota(jnp.int32, (8, u.shape[1]), 0)
            r1, r2 = pltpu.roll(u, 1, 0), pltpu.roll(u, 2, 0)
            top1 = jnp.where(row == 0, h[1:2, :], r1[0:8])
            top2 = jnp.where(row == 0, h[0:1, :], jnp.where(row == 1, h[1:2, :], r2[0:8]))
            u1 = jnp.concatenate([top1, r1[8:]], axis=0)
            u2 = jnp.concatenate([top2, r2[8:]], axis=0)
        c = b_ref[...] + wd_ref[0:1, :] * u2 + wd_ref[1:2, :] * u1 + wd_ref[2:3, :] * u
        new_h = u[tm - hrows:, :]
        halo[j] = new_h
        s_ref[j] = new_h
        return c

    ca = conv_half(wua_ref, wda_ref, ba_ref, halo_a, sa_ref)
    cb = conv_half(wub_ref, wdb_ref, bb_ref, halo_b, sb_ref)
    contrib = _dot((_silu(ca) * cb).astype(BF), wdn_ref[...])

    @pl.when(j == 0)
    def _():
        acc_scr[...] = x_ref[...] + contrib

    @pl.when(j > 0)
    def _():
        acc_scr[...] = acc_scr[...] + contrib

    @pl.when(j == pl.num_programs(2) - 1)
    def _():
        if final_norm:
            o_ref[...] = _rms(acc_scr[...], gf_ref[...])
        else:
            o_ref[...] = acc_scr[...]


def conv_ffn(x, g, hist, w_up, w_dw, b_dw, w_down, n_outer, tm, step, cw, final_g=None):
    m, d = x.shape
    d_ff = w_down.shape[0]
    rows = m // n_outer
    nt = rows // tm
    nj = d_ff // cw
    hrows = (FFN_WIDTH - 1) * step
    assert tm >= hrows and d_ff % cw == 0
    final_norm = final_g is not None

    in_specs = [
        pl.BlockSpec((tm, d), lambda o, t, j: (o * nt + t, 0)),
        pl.BlockSpec((1, d), lambda o, t, j: (0, 0)),
        pl.BlockSpec((None, hrows, cw), lambda o, t, j: (o, 0, j)),
        pl.BlockSpec((None, hrows, cw), lambda o, t, j: (o, 0, nj + j)),
        pl.BlockSpec((d, cw), lambda o, t, j: (0, j)),
        pl.BlockSpec((d, cw), lambda o, t, j: (0, nj + j)),
        pl.BlockSpec((FFN_WIDTH, cw), lambda o, t, j: (0, j)),
        pl.BlockSpec((FFN_WIDTH, cw), lambda o, t, j: (0, nj + j)),
        pl.BlockSpec((1, cw), lambda o, t, j: (0, j)),
        pl.BlockSpec((1, cw), lambda o, t, j: (0, nj + j)),
        pl.BlockSpec((cw, d), lambda o, t, j: (j, 0)),
    ]
    args = [x, g.reshape(1, d), hist, hist, w_up, w_up, w_dw, w_dw, b_dw.reshape(1, -1), b_dw.reshape(1, -1), w_down]
    if final_norm:
        in_specs.append(pl.BlockSpec((1, d), lambda o, t, j: (0, 0)))
        args.append(final_g.reshape(1, d))
    out, sa, sb = pl.pallas_call(
        functools.partial(_ffn_kernel, tm=tm, step=step, final_norm=final_norm),
        grid=(n_outer, nt, nj),
        in_specs=in_specs,
        out_specs=[pl.BlockSpec((tm, d), lambda o, t, j: (o * nt + t, 0)),
                   pl.BlockSpec((None, nj, hrows, cw), lambda o, t, j: (o, 0, 0, 0)),
                   pl.BlockSpec((None, nj, hrows, cw), lambda o, t, j: (o, 0, 0, 0))],
        out_shape=[jax.ShapeDtypeStruct((m, d), F32),
                   jax.ShapeDtypeStruct((n_outer, nj, hrows, cw), F32),
                   jax.ShapeDtypeStruct((n_outer, nj, hrows, cw), F32)],
        scratch_shapes=[pltpu.VMEM((tm, d), BF), pltpu.VMEM((tm, d), F32),
                        pltpu.VMEM((nj, hrows, cw), F32), pltpu.VMEM((nj, hrows, cw), F32)],
        compiler_params=_cparams("parallel", "arbitrary", "arbitrary"),
        name="conv_ffn",
    )(*args)
    def cols(z):
        return z.transpose(0, 2, 1, 3).reshape(n_outer, hrows, d_ff)

    return out, jnp.concatenate([cols(sa), cols(sb)], axis=-1)


def _compress_halves(x, pe2, w1cat):
    pw = _dot(pe2.astype(BF), w1cat)
    const = jnp.concatenate([pw[0:1, :CMP_HIDDEN], pw[1:2, CMP_HIDDEN:]], axis=1)
    ab = _dot(x.astype(BF), w1cat) + const
    return ab[:, :CMP_HIDDEN], ab[:, CMP_HIDDEN:]


def _compress_kernel(p_ref, pe_ref, w1_ref, w2_ref, o_ref, *, n):
    row = lax.broadcasted_iota(jnp.int32, (n, NSA_DH), 0)
    for g in range(NSA_GROUPS):
        a, b = _compress_halves(p_ref[g], pe_ref[...], w1_ref[...])
        h = _silu(a + pltpu.roll(b, n - 1, 0))
        o = _dot(h.astype(BF), w2_ref[...])
        o_ref[g] = jnp.where(row < n - 1, o, 0.0)


def compress(pieces, pe_ab, w1_cat, w2):
    bsz, _, _, n, pw = pieces.shape
    return pl.pallas_call(
        functools.partial(_compress_kernel, n=n),
        grid=(bsz, 2),
        in_specs=[pl.BlockSpec((None, None, NSA_GROUPS, n, pw), lambda b, c: (b, c, 0, 0, 0)),
                  pl.BlockSpec((None, 2, pw), lambda b, c: (c, 0, 0)),
                  pl.BlockSpec((None, pw, 2 * CMP_HIDDEN), lambda b, c: (c, 0, 0)),
                  pl.BlockSpec((None, CMP_HIDDEN, NSA_DH), lambda b, c: (c, 0, 0))],
        out_specs=pl.BlockSpec((None, None, NSA_GROUPS, n, NSA_DH), lambda b, c: (b, c, 0, 0, 0)),
        out_shape=jax.ShapeDtypeStruct((bsz, 2, NSA_GROUPS, n, NSA_DH), F32),
        compiler_params=_cparams("parallel", "parallel"),
        name="nsa_compress",
    )(pieces, pe_ab, w1_cat, w2)


def _bias_softmax(s, bias):
    s = s + bias
    m = jnp.max(s, axis=-1, keepdims=True)
    p = jnp.exp(s - m)
    inv = jnp.where(m > 0.5 * NEG, 1.0 / jnp.sum(p, axis=-1, keepdims=True), 0.0)
    return p * inv


def _stack_heads(x):
    return jnp.concatenate([x] * NSA_REP, axis=0)


def _sum_heads(p, tq):
    out = p[0:tq]
    for r in range(1, NSA_REP):
        out = out + p[r * tq:(r + 1) * tq]
    return out


def _sel_scores(imp, blk, tp, ns):
    cur = tp // SEL_BLOCK
    forced = (blk == 0) | (blk == cur) | (blk == cur - 1)
    valid = blk * SEL_BLOCK <= tp
    score = jnp.where(forced, 1e6, jnp.where(valid, imp, -1e6))
    return jnp.where(blk < ns, score, -3e6)


def _top_n_mask(score, blk, axis, n):
    sel = jnp.zeros(score.shape, F32)
    for _ in range(n):
        mx = jnp.max(score, axis=axis, keepdims=True)
        first = jnp.min(jnp.where(score == mx, blk, LANES), axis=axis, keepdims=True)
        pick = blk == first
        sel = jnp.where(pick, 1.0, sel)
        score = jnp.where(pick, -3e38, score)
    return sel


def _gate_cols(gt, col0, br):
    return jnp.concatenate([gt[:, col0 + r * 3 + br:col0 + r * 3 + br + 1] for r in range(NSA_REP)], axis=0)


def _softmax_keys_on_rows(s):
    m = jnp.max(s, axis=0, keepdims=True)
    p = jnp.exp(s - m)
    inv = jnp.where(m > 0.5 * NEG, 1.0 / jnp.sum(p, axis=0, keepdims=True), 0.0)
    return p * inv


def _nsa_prompt_kernel(q_ref, gt_ref, kc_ref, vct_ref, ks_ref, vst_ref, kw_ref, vwt_ref, o_ref, bsel_scr,
                       *, tq, ncp, ns, ck, nw):
    rep, dh = NSA_REP, NSA_DH
    qb = pl.program_id(2) * tq
    n_chunks = (qb + tq - 1) // ck + 1
    ws = pl.multiple_of(jnp.maximum(qb - WINDOW, 0), tq)
    vb = vst_ref.shape[2]
    wbk = vwt_ref.shape[2]
    bpk = ck // SEL_BLOCK

    qt = (q_ref[...] * (dh ** -0.5)).T
    qt_cat = jnp.concatenate([qt[r * dh:(r + 1) * dh] for r in range(rep)], axis=1).astype(BF)

    def heads(x):
        return jnp.concatenate([x] * rep, axis=1)

    def tpos(shape):
        return lax.broadcasted_iota(jnp.int32, shape, 1) + qb

    n_idx = lax.broadcasted_iota(jnp.int32, (ncp, tq), 0)
    bias_c = jnp.where(CMP_STRIDE * n_idx + CMP_LEN - 1 <= tpos((ncp, tq)), 0.0, NEG)
    p_c = _softmax_keys_on_rows(_dot(kc_ref[...], qt_cat) + heads(bias_c))
    o_c = _dot(vct_ref[...], p_c.astype(BF))
    psum = p_c[:, 0:tq]
    for r in range(1, rep):
        psum = psum + p_c[:, r * tq:(r + 1) * tq]

    wpos = lax.broadcasted_iota(jnp.int32, (nw, tq), 0) + ws
    tw = tpos((nw, tq))
    bias_w = jnp.where((wpos <= tw) & (wpos > tw - WINDOW), 0.0, NEG)
    p_w = _softmax_keys_on_rows(_dot(kw_ref[pl.ds(ws, nw), :], qt_cat) + heads(bias_w)).astype(BF)
    wblk = ws // wbk
    o_w = _dot(vwt_ref[wblk], p_w[0:wbk])
    for jb in range(1, nw // wbk):
        o_w = o_w + _dot(vwt_ref[wblk + jb], p_w[jb * wbk:(jb + 1) * wbk])

    cj = lax.broadcasted_iota(jnp.int32, (LANES, ncp), 0) * SEL_BLOCK
    cn = lax.broadcasted_iota(jnp.int32, (LANES, ncp), 1) * CMP_STRIDE
    cover_t = jnp.where((cn <= cj + SEL_BLOCK - 1) & (cn + CMP_LEN - 1 >= cj), 1.0, 0.0).astype(BF)
    hi, mid, lo = _split3(psum)
    imp_t = _dot(cover_t, hi) + _dot(cover_t, mid) + _dot(cover_t, lo)
    blk = lax.broadcasted_iota(jnp.int32, (LANES, tq), 0)
    sel_t = _top_n_mask(_sel_scores(imp_t, blk, tpos((LANES, tq)), ns), blk, 0, min(SEL_TOPN, ns))
    bsel_scr[...] = jnp.where(sel_t > 0.5, 0.0, NEG)

    def sel_chunk(c, carry, causal):
        m, acc = carry
        s = _dot(ks_ref[pl.ds(pl.multiple_of(c * ck, ck), ck), :], qt_cat)
        brows = heads(bsel_scr[pl.ds(pl.multiple_of(c * bpk, bpk), bpk), :])
        s = jnp.concatenate([s[j * SEL_BLOCK:(j + 1) * SEL_BLOCK] + brows[j:j + 1] for j in range(bpk)], axis=0)
        if causal:
            kpos = lax.broadcasted_iota(jnp.int32, (ck, tq), 0) + c * ck
            s = s + heads(jnp.where(kpos <= tpos((ck, tq)), 0.0, NEG))
        m_new = jnp.maximum(m, jnp.max(s, axis=0, keepdims=True))
        p = jnp.exp(s - m_new).astype(BF)
        acc = jnp.exp(m - m_new) * acc
        for jb in range(ck // vb):
            acc = acc + _dot(vst_ref[c * (ck // vb) + jb], p[jb * vb:(jb + 1) * vb])
        return m_new, acc

    init = (jnp.full((1, rep * tq), NEG, F32), jnp.zeros((vst_ref.shape[1], rep * tq), F32))
    fin = lax.fori_loop(0, n_chunks - 1, functools.partial(sel_chunk, causal=False), init)
    _, acc_s = sel_chunk(n_chunks - 1, fin, True)
    o_s = acc_s[:dh] / jnp.maximum(acc_s[dh:dh + 1], 1e-30)

    gt = _sigmoid(gt_ref[...])

    def gate(br):
        return jnp.concatenate([gt[3 * r + br:3 * r + br + 1] for r in range(rep)], axis=1)

    o_all = gate(0) * o_c + gate(1) * o_s + gate(2) * o_w
    o_ref[...] = jnp.concatenate([o_all[:, r * tq:(r + 1) * tq] for r in range(rep)], axis=0).T


def nsa_attn_prompt(q, gates_t, kc, vct, ks, vst, kw, vwt, *, tq, ck):
    bsz, t_len, _ = q.shape
    ncp = kc.shape[2]
    gw = NSA_REP * NSA_DH
    ns = -(-t_len // SEL_BLOCK)
    vb, wbk = vst.shape[-1], vwt.shape[-1]
    assert tq == LANES and ck % vb == 0 and tq % wbk == 0 and (ck // SEL_BLOCK) % 8 == 0 and ck % tq == 0
    assert ns <= LANES and t_len % ck == 0 and t_len >= WINDOW + tq
    ones_rows = 16
    ones = jnp.concatenate([jnp.ones(vst.shape[:3] + (1, vb), BF),
                            jnp.zeros(vst.shape[:3] + (ones_rows - 1, vb), BF)], axis=3)
    vst = jnp.concatenate([vst, ones], axis=3)

    def blk_spec(x):
        return pl.BlockSpec((None, None) + x.shape[2:], lambda b, g, i: (b, g) + (0,) * (x.ndim - 2))

    return pl.pallas_call(
        functools.partial(_nsa_prompt_kernel, tq=tq, ncp=ncp, ns=ns, ck=ck, nw=WINDOW + tq),
        scratch_shapes=[pltpu.VMEM((LANES, tq), F32)],
        grid=(bsz, NSA_GROUPS, t_len // tq),
        in_specs=[pl.BlockSpec((None, tq, gw), lambda b, g, i: (b, i, g)),
                  pl.BlockSpec((None, None, NSA_REP * 3, tq), lambda b, g, i: (b, g, 0, i)),
                  blk_spec(kc), blk_spec(vct), blk_spec(ks), blk_spec(vst), blk_spec(kw), blk_spec(vwt)],
        out_specs=pl.BlockSpec((None, tq, gw), lambda b, g, i: (b, i, g)),
        out_shape=jax.ShapeDtypeStruct((bsz, t_len, NSA_GROUPS * gw), F32),
        compiler_params=_cparams("parallel", "parallel", "arbitrary"),
        name="nsa_attn_prompt",
    )(q, gates_t, kc, vct, ks, vst, kw, vwt)


def _nsa_sample_kernel(pt_ref, q_ref, gt_ref, kvn_ref, wn_ref, *rest, n_pages, tq, past_len, wb):
    del pt_ref
    cmp_refs, sel_refs = rest[:n_pages], rest[n_pages:2 * n_pages]
    win_ref, pe_ref, w1_ref, w2_ref, o_ref = rest[2 * n_pages:]
    rep, dh, ng = NSA_REP, NSA_DH, NSA_GROUPS
    rows = rep * tq
    seg = PAGE_SIZE
    ppp = PAGE_SIZE // CMP_STRIDE
    ncp = n_pages * ppp
    nk = past_len + seg
    ns = -(-(past_len + tq) // SEL_BLOCK)

    def tpos(shape):
        return (lax.broadcasted_iota(jnp.int32, shape, 0) & (tq - 1)) + past_len

    kcvc = []
    for c in range(2):
        pieces = jnp.concatenate([cmp_refs[p][c, g] for g in range(ng) for p in range(n_pages)], axis=0)
        a, b = _compress_halves(pieces, pe_ref[c], w1_ref[c])
        per_group = []
        for g in range(ng):
            h = _silu(a[g * ncp:(g + 1) * ncp] + pltpu.roll(b[g * ncp:(g + 1) * ncp], ncp - 1, 0))
            per_group.append(_dot(h.astype(BF), w2_ref[c]).astype(BF))
        kcvc.append(per_group)

    q = q_ref[...] * (dh ** -0.5)
    kvn = kvn_ref[...]
    wn = wn_ref[...]
    kvw = ng * dh

    def q4_of(g):
        return jnp.concatenate([q[:, (g * rep + r) * dh:(g * rep + r + 1) * dh] for r in range(rep)],
                               axis=0).astype(BF)

    def new_seg(x, col):
        return jnp.concatenate([x[:, col:col + dh], jnp.zeros((seg - tq, dh), F32)], axis=0).astype(BF)

    n_idx = lax.broadcasted_iota(jnp.int32, (rows, ncp), 1)
    bias_c = jnp.where(CMP_STRIDE * n_idx + CMP_LEN - 1 <= tpos((rows, ncp)), 0.0, NEG)
    q4s, o_cs, psums = [], [], []
    for g in range(ng):
        q4 = q4_of(g)
        p_c = _bias_softmax(_dot_nt(q4, kcvc[0][g]), bias_c)
        q4s.append(q4)
        o_cs.append(_dot(p_c.astype(BF), kcvc[1][g]))
        psums.append(_sum_heads(p_c, tq))
    cj = lax.broadcasted_iota(jnp.int32, (LANES, ncp), 0) * SEL_BLOCK
    cn = lax.broadcasted_iota(jnp.int32, (LANES, ncp), 1) * CMP_STRIDE
    cover_t = jnp.where((cn <= cj + SEL_BLOCK - 1) & (cn + CMP_LEN - 1 >= cj), 1.0, 0.0).astype(BF)
    hi, mid, lo = _split3(jnp.concatenate(psums, axis=0))
    imp_t = _dot_nt(cover_t, hi) + _dot_nt(cover_t, mid) + _dot_nt(cover_t, lo)
    blk = lax.broadcasted_iota(jnp.int32, (LANES, ng * tq), 0)
    tp_t = (lax.broadcasted_iota(jnp.int32, (LANES, ng * tq), 1) & (tq - 1)) + past_len
    sel_t = _top_n_mask(_sel_scores(imp_t, blk, tp_t, ns), blk, 0, min(SEL_TOPN, ns))
    ej = lax.broadcasted_iota(jnp.int32, (LANES, nk), 0)
    ek = lax.broadcasted_iota(jnp.int32, (LANES, nk), 1)
    expand = jnp.where(ek // SEL_BLOCK == ej, 1.0, 0.0).astype(BF)
    sel_cols = jnp.concatenate([sel_t[:, g * tq:(g + 1) * tq] for g in range(ng) for _ in range(rep)], axis=1)
    mk_all = _dot_tn(sel_cols.astype(BF), expand)
    kpos = lax.broadcasted_iota(jnp.int32, (rows, nk), 1)
    causal = kpos <= tpos((rows, nk))
    wp = lax.broadcasted_iota(jnp.int32, (rows, wb + seg), 1) + (past_len - wb)
    tw = tpos((rows, wb + seg))
    bias_w = jnp.where((wp <= tw) & (wp > tw - WINDOW), 0.0, NEG)

    gt = _sigmoid(gt_ref[...])
    for g in range(ng):
        q4 = q4s[g]
        sc = jnp.concatenate([_dot(q4, sel_refs[p][0, g]) for p in range(n_pages)]
                             + [_dot_nt(q4, new_seg(kvn, 2 * kvw + g * dh))], axis=1)
        bias = jnp.where(causal & (mk_all[g * rows:(g + 1) * rows] > 0.5), 0.0, NEG)
        p_s = _bias_softmax(sc, bias).astype(BF)
        o_s = _dot(p_s[:, past_len:], new_seg(kvn, 3 * kvw + g * dh))
        for p in range(n_pages):
            o_s = o_s + _dot_nt(p_s[:, p * seg:(p + 1) * seg], sel_refs[p][1, g])
        sw = jnp.concatenate([_dot(q4, win_ref[0, g]), _dot_nt(q4, new_seg(wn, g * dh))], axis=1)
        p_w = _bias_softmax(sw, bias_w).astype(BF)
        o_w = _dot_nt(p_w[:, :wb], win_ref[1, g]) + _dot(p_w[:, wb:], new_seg(wn, kvw + g * dh))
        c0 = g * rep * 3
        o4 = _gate_cols(gt, c0, 0) * o_cs[g] + _gate_cols(gt, c0, 1) * o_s + _gate_cols(gt, c0, 2) * o_w
        o_ref[:, g * rep * dh:(g + 1) * rep * dh] = jnp.concatenate(
            [o4[r * tq:(r + 1) * tq] for r in range(rep)], axis=1)


def nsa_attn_sample(q, gates, kv_new, win_new, cmp_pages, sel_pages, win_t, page_table, pe_ab, w1_cat, w2):
    bsz, tq, hq = q.shape
    n_pages = page_table.shape[1]
    wb = win_t.shape[-1]
    assert tq & (tq - 1) == 0 and tq <= PAGE_SIZE

    def row_spec(w):
        return pl.BlockSpec((None, tq, w), lambda b, pt: (b, 0, 0))

    def page_spec(shape, p):
        return pl.BlockSpec((None,) + shape, lambda b, pt: (pt[b, p],) + (0,) * len(shape))

    def full_spec(x):
        return pl.BlockSpec(x.shape, lambda b, pt: (0,) * x.ndim)

    in_specs = ([row_spec(hq), row_spec(gates.shape[2]), row_spec(kv_new.shape[2]), row_spec(win_new.shape[2])]
                + [page_spec(cmp_pages.shape[1:], p) for p in range(n_pages)]
                + [page_spec(sel_pages.shape[1:], p) for p in range(n_pages)]
                + [pl.BlockSpec((None,) + win_t.shape[1:], lambda b, pt: (b, 0, 0, 0, 0)),
                   full_spec(pe_ab), full_spec(w1_cat), full_spec(w2)])
    return pl.pallas_call(
        functools.partial(_nsa_sample_kernel, n_pages=n_pages, tq=tq, past_len=n_pages * PAGE_SIZE, wb=wb),
        grid_spec=pltpu.PrefetchScalarGridSpec(
            num_scalar_prefetch=1, grid=(bsz,), in_specs=in_specs, out_specs=row_spec(hq)),
        out_shape=jax.ShapeDtypeStruct((bsz, tq, hq), F32),
        compiler_params=_cparams("parallel"),
        name="nsa_attn_sample",
    )(page_table, q, gates, kv_new, win_new, *([cmp_pages] * n_pages), *([sel_pages] * n_pages),
      win_t, pe_ab, w1_cat, w2)


def _pad_cols(w, n):
    return jnp.pad(w, ((0, 0), (0, n - w.shape[1])))


def _group_major(x, bsz, t_len):
    return x.reshape(bsz, t_len, NSA_GROUPS, NSA_DH).transpose(0, 2, 1, 3)


def _group_major_t(x, bsz, t_len, blk):
    x5 = x.reshape(bsz, t_len // blk, blk, NSA_GROUPS, NSA_DH)
    return x5.transpose(0, 3, 1, 4, 2)


def _ffn_chunk(d_ff):
    n = d_ff // LANES
    for parts in range(2, n + 1):
        if n % parts == 0:
            return (n // parts) * LANES
    return d_ff


def _to_time_major(x, bsz, t_len):
    return x.reshape(bsz, t_len, -1).transpose(1, 0, 2).reshape(t_len * bsz, -1)


def _to_batch_major(x, bsz, t_len):
    return x.reshape(t_len, bsz, -1).transpose(1, 0, 2).reshape(bsz * t_len, -1)


def kernel(x_prompt, x_sample, cache_gla_state, cache_conv, cache_nsa_kv, cache_nsa_win, cache_mem_kv,
           cache_ffn_conv, page_table, mem_prompt, norm_mix, norm_mem, norm_x, norm_ffn, norm_final,
           w_in_a, w_gate_a, b_gate_a, g_gla_out, w_dw_b, b_dw_b, g_ln_b, b_ln_b, w_out_a,
           w_in_c, pe_cmp, w_cmp1, w_cmp2, w_out_c, w_xq, w_mem_kv, w_xo, w_up, w_ffn_dw, b_ffn_dw, w_down):
    bp, tp, d = x_prompt.shape
    bs, ts, _ = x_sample.shape
    depth = norm_mix.shape[0]
    n_mem = mem_prompt.shape[1]
    d_ff = w_down.shape[1]
    hk, hv = GLA_HEADS * GLA_DK, GLA_HEADS * GLA_DV
    kvw = NSA_GROUPS * NSA_DH
    hq = NSA_HEADS * NSA_DH

    xp = x_prompt.reshape(bp * tp, d)
    xs = x_sample.reshape(bs * ts, d)
    tm_p = 512
    tm_s = bs * ts

    outs = {k: [] for k in ("gla_p", "gla_s", "conv_p", "conv_s", "nsa_p", "nsa_s", "win_p", "win_s",
                            "mem_p", "ffn_p", "ffn_s")}
    for l in range(depth):
        i = l // 2
        if l % 2 == 0:
            wi = w_in_a[i]
            c0, c1, c2, c3, c4 = hk, 2 * hk, 2 * hk + hv, 2 * hk + 2 * hv, 2 * hk + 2 * hv + GLA_RANK
            w_a = jnp.concatenate([wi[:, c4:], wi[:, c1:c2], wi[:, c2:c3], wi[:, :c0], wi[:, c0:c1],
                                   _pad_cols(wi[:, c3:c4], LANES)], axis=1).astype(BF)
            splits = (2 * CONV_CH, hv, hv, hk, hk, LANES)
            wg = jnp.pad(w_gate_a[i], ((0, LANES - GLA_RANK), (0, 0))).astype(BF)
            bg = b_gate_a[i].reshape(1, hk)
            go = g_gla_out[i].reshape(1, hv)
            w_out = w_out_a[i].astype(BF)
            vecs = [v.reshape(1, CONV_CH) for v in (b_dw_b[i], g_ln_b[i], b_ln_b[i])]

            def mixer(x, bsz, t_len, tm, s0, hist_tm, n_outer, ctm, step):
                u, v, r, q, k, a = norm_matmul(x, norm_mix[l], w_a, splits, tm)
                sh = lambda z: z.reshape(bsz, t_len, -1)
                og, s_new = gla(sh(q), sh(k), sh(v), sh(r), sh(a), wg, bg, go, s0, 1024)
                if step > 1:
                    u = _to_time_major(u, bsz, t_len)
                c, hist_new = convmod(u, hist_tm, w_dw_b[i], *vecs, n_outer, ctm, step)
                if step > 1:
                    c = _to_batch_major(c, bsz, t_len)
                y = matmul_res([og.reshape(bsz * t_len, hv), c], [w_out[:hv], w_out[hv:]], x, tm)
                return y, s_new, hist_new

            xp, sp, cp = mixer(xp, bp, tp, tm_p, jnp.zeros((bp, GLA_HEADS, GLA_DK, GLA_DV), F32),
                               jnp.zeros((bp, CONV_WIDTH - 1, CONV_CH), F32), bp, 512, 1)
            hist_s = cache_conv[i].transpose(1, 0, 2).reshape(1, (CONV_WIDTH - 1) * bs, CONV_CH)
            xs, ss, cs = mixer(xs, bs, ts, tm_s, cache_gla_state[i], hist_s, 1, bs * ts, bs)
            cs = cs.reshape(CONV_WIDTH - 1, bs, CONV_CH).transpose(1, 0, 2)
            outs["gla_p"].append(sp)
            outs["gla_s"].append(ss)
            outs["conv_p"].append(cp)
            outs["conv_s"].append(cs)
        else:
            n_in = hq + 6 * kvw + 3 * NSA_HEADS
            n_pad = -(-n_in // LANES) * LANES
            w_c = _pad_cols(w_in_c[i], n_pad).astype(BF)
            splits = (hq, 4 * kvw, 2 * kvw, n_pad - hq - 6 * kvw)
            w_out = w_out_c[i].astype(BF)
            pw = CMP_STRIDE * NSA_DH
            pe_ab = pe_cmp[i].reshape(2, 2, pw)
            w1_ab = w_cmp1[i].reshape(2, 2, pw, CMP_HIDDEN)
            w1_cat = jnp.concatenate([w1_ab[:, 0], w1_ab[:, 1]], axis=-1).astype(BF)
            w2 = w_cmp2[i].astype(BF)

            def pieces_of(rows, bsz, n):
                r6 = rows[:, :n * CMP_STRIDE].reshape(bsz, n, CMP_STRIDE, 2, NSA_GROUPS, NSA_DH)
                return r6.transpose(0, 3, 4, 1, 2, 5).reshape(bsz, 2, NSA_GROUPS, n, pw).astype(BF)

            q, kv4, win, gl = norm_matmul(xp, norm_mix[l], w_c, splits, tm_p)
            kv4b = kv4.reshape(bp, tp, 4 * kvw)
            kcvc = compress(pieces_of(kv4b[:, :, :2 * kvw], bp, tp // CMP_STRIDE), pe_ab, w1_cat, w2)
            ks = _group_major(kv4[:, 2 * kvw:3 * kvw], bp, tp).astype(BF)
            vst = _group_major_t(kv4[:, 3 * kvw:], bp, tp, 2 * LANES).astype(BF)
            kw = _group_major(win[:, :kvw], bp, tp).astype(BF)
            vwt = _group_major_t(win[:, kvw:], bp, tp, LANES).astype(BF)
            gates_t = gl[:, :3 * NSA_HEADS].reshape(bp, tp, NSA_GROUPS, NSA_REP * 3).transpose(0, 2, 3, 1)
            o = nsa_attn_prompt(q.reshape(bp, tp, hq), gates_t, kcvc[:, 0].astype(BF),
                                kcvc[:, 1].transpose(0, 1, 3, 2).astype(BF), ks, vst, kw, vwt,
                                tq=Q_BLOCK, ck=1024)
            xp = matmul_res([o.reshape(bp * tp, hq)], [w_out], xp, tm_p)
            win_rows = min(WINDOW, tp)
            outs["nsa_p"].append(kv4.reshape(bp, tp, 4, NSA_GROUPS, NSA_DH))
            outs["win_p"].append(win.reshape(bp, tp, 2, NSA_GROUPS, NSA_DH)[:, tp - win_rows:])

            q, kv4, win, gl = norm_matmul(xs, norm_mix[l], w_c, splits, tm_s)
            pool = cache_nsa_kv[i]
            n_phys = pool.shape[0]
            ppp = PAGE_SIZE // CMP_STRIDE
            cmp_pages = (pool[:, :, :2].reshape(n_phys, ppp, CMP_STRIDE, 2, NSA_GROUPS, NSA_DH)
                         .transpose(0, 3, 4, 1, 2, 5).reshape(n_phys, 2, NSA_GROUPS, ppp, pw))
            sel_pages = pool[:, :, 2:].transpose(0, 2, 3, 4, 1).astype(BF)
            win_cache = cache_nsa_win[i]
            wb = win_cache.shape[1]
            win_t = win_cache.transpose(0, 2, 3, 4, 1).astype(BF)
            o = nsa_attn_sample(q.reshape(bs, ts, hq), gl.reshape(bs, ts, -1), kv4.reshape(bs, ts, 4 * kvw),
                                win.reshape(bs, ts, 2 * kvw), cmp_pages, sel_pages, win_t, page_table,
                                pe_ab, w1_cat, w2)
            xs = matmul_res([o.reshape(bs * ts, hq)], [w_out], xs, tm_s)
            outs["nsa_s"].append(kv4.reshape(bs, ts, 4, NSA_GROUPS, NSA_DH))
            win_new = win.reshape(bs, ts, 2, NSA_GROUPS, NSA_DH)
            outs["win_s"].append(jnp.concatenate([win_cache, win_new], axis=1)[:, ts:])

        xhw = X_HEADS * X_DH
        (mkv,) = norm_matmul(mem_prompt.reshape(bp * n_mem, d), norm_mem[l], w_mem_kv[l].astype(BF),
                             (2 * xhw,), bp * n_mem)
        outs["mem_p"].append(mkv.reshape(bp, n_mem, 2, X_HEADS, X_DH))
        w_q = w_xq[l].astype(BF)
        w_o = w_xo[l].astype(BF)
        (qx,) = norm_matmul(xp, norm_x[l], w_q, (xhw,), tm_p)
        ox = xattn(qx.reshape(bp, tp, xhw), mkv.reshape(1, bp, n_mem, 2, X_HEADS, X_DH), 0, 512)
        xp = matmul_res([ox.reshape(bp * tp, xhw)], [w_o], xp, tm_p)
        (qx,) = norm_matmul(xs, norm_x[l], w_q, (xhw,), tm_s)
        ox = xattn(qx.reshape(bs, ts, xhw), cache_mem_kv, l, ts)
        xs = matmul_res([ox.reshape(bs * ts, xhw)], [w_o], xs, tm_s)

        wu = w_up[l].astype(BF)
        wd = w_down[l].astype(BF)
        fg = norm_final if l == depth - 1 else None
        cw = _ffn_chunk(d_ff)
        xp, hfp = conv_ffn(xp, norm_ffn[l], jnp.zeros((bp, FFN_WIDTH - 1, 2 * d_ff), F32), wu, w_ffn_dw[l],
                           b_ffn_dw[l], wd, bp, 512, 1, cw, fg)
        hist_s = cache_ffn_conv[l].transpose(1, 0, 2).reshape(1, (FFN_WIDTH - 1) * bs, 2 * d_ff)
        xs_tm, hfs = conv_ffn(_to_time_major(xs, bs, ts), norm_ffn[l], hist_s, wu, w_ffn_dw[l], b_ffn_dw[l], wd,
                              1, bs * ts, bs, cw, fg)
        xs = _to_batch_major(xs_tm, bs, ts)
        outs["ffn_p"].append(hfp)
        outs["ffn_s"].append(hfs.reshape(FFN_WIDTH - 1, bs, 2 * d_ff).transpose(1, 0, 2))

    st = jnp.stack
    return (xp.reshape(bp, tp, d), xs.reshape(bs, ts, d),
            st(outs["gla_p"]), st(outs["gla_s"]), st(outs["conv_p"]), st(outs["conv_s"]),
            st(outs["nsa_p"]), st(outs["nsa_s"]), st(outs["win_p"]), st(outs["win_s"]),
            st(outs["mem_p"]), st(outs["ffn_p"]), st(outs["ffn_s"]))
```
